```python
import jax, jax.numpy as jnp
from jax import lax
import numpy as np

D_MODEL = 4096
BATCH = 2
SEQ = 8192
DEPTH = 2

N_MIXERS = 2
N_LRU_LAYERS = (DEPTH + 1) // 2
N_FOX_LAYERS = DEPTH // 2
LRU_WIDTH = D_MODEL
LRU_HEADS = 16
LRU_BLOCK = LRU_WIDTH // LRU_HEADS
CONV_WIDTH = 4
LRU_C = 8.0
HEAD_DIM = 128
N_HEADS = D_MODEL // HEAD_DIM
Q_BLOCK = 128
ATTN_SCALE = HEAD_DIM ** -0.5
N_EXPERTS = 32
TOP_K = 4
D_EXPERT = 768
SWIGLU_LIMIT = 7.0
SWIGLU_ALPHA = 1.702
ROW_BLOCK = 128
N_MOD = 6
EPS = 1e-6

kernel_name = "hybrid_rglru_fox_moe_adaln"


def rms_norm(x, gain):
    xf = x.astype(jnp.float32)
    y = xf * lax.rsqrt(jnp.mean(xf * xf, axis=-1, keepdims=True) + EPS)
    return (y * gain.astype(jnp.float32)).astype(x.dtype)


def modulate(h, shift, scale):
    return h * (1 + scale[:, None, :]) + shift[:, None, :]


def causal_depthwise_conv(x, w, b):
    y = lax.conv_general_dilated(
        x, w[:, None, :].astype(x.dtype), window_strides=(1,),
        padding=[(CONV_WIDTH - 1, 0)], dimension_numbers=('NWC', 'WIO', 'NWC'),
        feature_group_count=x.shape[-1])
    return y + b


def _linear_combine(e1, e2):
    a1, b1 = e1
    a2, b2 = e2
    return a1 * a2, a2 * b1 + b2


def rglru_mixer(h, w_in, conv_w, conv_b, w_a, b_a, w_x, b_x, lam, w_out):
    B, S, _ = h.shape
    u = h @ w_in
    xb, gb = u[..., :LRU_WIDTH], u[..., LRU_WIDTH:]
    xc = causal_depthwise_conv(xb, conv_w, conv_b)
    xh = xc.reshape(B, S, LRU_HEADS, LRU_BLOCK)
    r = jax.nn.sigmoid(jnp.einsum('bshi,hij->bshj', xh, w_a).reshape(B, S, LRU_WIDTH) + b_a)
    i = jax.nn.sigmoid(jnp.einsum('bshi,hij->bshj', xh, w_x).reshape(B, S, LRU_WIDTH) + b_x)
    log_a = -LRU_C * r.astype(jnp.float32) * jax.nn.softplus(-lam.astype(jnp.float32))
    a = jnp.exp(log_a)
    mult = jnp.sqrt(-jnp.expm1(2.0 * log_a))
    bt = mult * (i * xc).astype(jnp.float32)
    _, hs = lax.associative_scan(_linear_combine, (a, bt), axis=1)
    y = hs.astype(h.dtype) * jax.nn.gelu(gb)
    return y @ w_out


def fox_mixer(h, w_in, b_f, q_gain, k_gain, w_out):
    B, S, D = h.shape
    u = h @ w_in
    q = u[..., :D].reshape(B, S, N_HEADS, HEAD_DIM)
    k = u[..., D:2 * D].reshape(B, S, N_HEADS, HEAD_DIM)
    v = u[..., 2 * D:3 * D].reshape(B, S, N_HEADS, HEAD_DIM)
    f_logit = u[..., 3 * D:]
    q = rms_norm(q, q_gain).transpose(0, 2, 1, 3)
    k = rms_norm(k, k_gain).transpose(0, 2, 1, 3)
    v = v.transpose(0, 2, 1, 3)
    log_f = jax.nn.log_sigmoid(f_logit.astype(jnp.float32) + b_f.astype(jnp.float32))
    cum = jnp.cumsum(log_f.transpose(0, 2, 1), axis=-1)
    n_blk = S // Q_BLOCK
    q_blocks = q.reshape(B, N_HEADS, n_blk, Q_BLOCK, HEAD_DIM).transpose(2, 0, 1, 3, 4)
    c_blocks = cum.reshape(B, N_HEADS, n_blk, Q_BLOCK).transpose(2, 0, 1, 3)
    pos = jnp.arange(S, dtype=jnp.int32)
    p_blocks = pos.reshape(n_blk, Q_BLOCK)

    def attend(blk):
        qb, cq, qpos = blk
        s = jnp.einsum('bhqd,bhkd->bhqk', qb, k, preferred_element_type=jnp.float32) * ATTN_SCALE
        s = s + (cq[..., :, None] - cum[..., None, :])
        s = jnp.where(pos[None, :] <= qpos[:, None], s, -jnp.inf)
        p = jax.nn.softmax(s, axis=-1)
        return jnp.einsum('bhqk,bhkd->bhqd', p.astype(v.dtype), v)

    o = lax.map(attend, (q_blocks, c_blocks, p_blocks))
    o = o.transpose(1, 0, 3, 2, 4).reshape(B, S, D)
    return o @ w_out


def moe_ffn(h, w_r, b_r, w_gu, b_gu, w_down, b_down):
    B, S, D = h.shape
    T = B * S
    A = T * TOP_K
    x2 = h.reshape(T, D)
    logits = jnp.einsum('td,de->te', x2, w_r, preferred_element_type=jnp.float32) + b_r.astype(jnp.float32)
    top_val, top_idx = lax.top_k(logits, TOP_K)
    gates = jax.nn.softmax(top_val, axis=-1)
    flat_e = top_idx.reshape(A)
    order = jnp.argsort(flat_e)
    e_sorted = flat_e[order]
    tok_sorted = (order // TOP_K).astype(jnp.int32)
    gate_sorted = gates.reshape(A)[order]
    counts = jnp.bincount(flat_e, length=N_EXPERTS)
    padded = (counts + ROW_BLOCK - 1) // ROW_BLOCK * ROW_BLOCK
    group_start = jnp.cumsum(counts) - counts
    padded_end = jnp.cumsum(padded)
    padded_start = padded_end - padded
    dest = padded_start[e_sorted] + jnp.arange(A) - group_start[e_sorted]
    n_rows = ((A + ROW_BLOCK - 1) // ROW_BLOCK) * ROW_BLOCK + N_EXPERTS * ROW_BLOCK
    n_blk = n_rows // ROW_BLOCK
    row_tok = jnp.full((n_rows,), T, jnp.int32).at[dest].set(tok_sorted)
    row_gate = jnp.zeros((n_rows,), jnp.float32).at[dest].set(gate_sorted)
    blk_expert = jnp.minimum(
        jnp.searchsorted(padded_end, jnp.arange(n_blk) * ROW_BLOCK, side='right'), N_EXPERTS - 1)
    x_pad = jnp.concatenate([x2, jnp.zeros((1, D), x2.dtype)], axis=0)

    def expert_block(args):
        tok, e = args
        xb = x_pad[tok]
        gu = xb @ w_gu[e] + b_gu[e]
        g = jnp.minimum(gu[:, :D_EXPERT], SWIGLU_LIMIT)
        up = jnp.clip(gu[:, D_EXPERT:], -SWIGLU_LIMIT, SWIGLU_LIMIT)
        act = (up + 1) * (g * jax.nn.sigmoid(SWIGLU_ALPHA * g))
        return act @ w_down[e] + b_down[e]

    y = lax.map(expert_block, (row_tok.reshape(n_blk, ROW_BLOCK), blk_expert))
    y = y.reshape(n_rows, D) * row_gate[:, None].astype(y.dtype)
    out = jax.ops.segment_sum(y, row_tok, num_segments=T + 1)[:T]
    return out.reshape(B, S, D)


def setup_inputs(seed: int = 0) -> dict:
    key = jax.random.key(seed)
    ks = iter(jax.random.split(key, 40))

    def nrm(shape, std):
        return jax.random.normal(next(ks), shape, jnp.float32) * std

    D = D_MODEL
    inp = {}
    inp["x"] = nrm((BATCH, SEQ, D), 1.0)
    inp["c"] = nrm((BATCH, D), 1.0)
    inp["ada_w"] = nrm((D, N_MOD * D), 0.5 * D ** -0.5)
    inp["ada_b"] = nrm((N_MOD * D,), 0.01)
    inp["ada_table"] = nrm((DEPTH, N_MOD, D), 0.1)
    inp["norm_mix_g"] = 1.0 + nrm((DEPTH, D), 0.02)
    inp["norm_ffn_g"] = 1.0 + nrm((DEPTH, D), 0.02)
    NA = N_LRU_LAYERS
    inp["lru_w_in"] = nrm((NA, D, 2 * LRU_WIDTH), D ** -0.5)
    inp["lru_conv_w"] = nrm((NA, CONV_WIDTH, LRU_WIDTH), CONV_WIDTH ** -0.5)
    inp["lru_conv_b"] = nrm((NA, LRU_WIDTH), 0.01)
    inp["lru_w_a"] = nrm((NA, LRU_HEADS, LRU_BLOCK, LRU_BLOCK), LRU_BLOCK ** -0.5)
    inp["lru_b_a"] = nrm((NA, LRU_WIDTH), 0.01)
    inp["lru_w_x"] = nrm((NA, LRU_HEADS, LRU_BLOCK, LRU_BLOCK), LRU_BLOCK ** -0.5)
    inp["lru_b_x"] = nrm((NA, LRU_WIDTH), 0.01)
    a_c = jax.random.uniform(next(ks), (NA, LRU_WIDTH), jnp.float32, 0.9, 0.999)
    a0 = a_c ** (1.0 / LRU_C)
    inp["lru_lam"] = jnp.log(a0) - jnp.log1p(-a0)
    inp["lru_w_out"] = nrm((NA, LRU_WIDTH, D), LRU_WIDTH ** -0.5)
    NB = N_FOX_LAYERS
    inp["fox_w_in"] = nrm((NB, D, 3 * D + N_HEADS), D ** -0.5)
    inp["fox_b_f"] = 2.0 + nrm((NB, N_HEADS), 0.5)
    inp["fox_q_gain"] = 1.0 + nrm((NB, HEAD_DIM), 0.02)
    inp["fox_k_gain"] = 1.0 + nrm((NB, HEAD_DIM), 0.02)
    inp["fox_w_out"] = nrm((NB, D, D), D ** -0.5)
    inp["moe_w_router"] = nrm((DEPTH, D, N_EXPERTS), D ** -0.5)
    inp["moe_b_router"] = nrm((DEPTH, N_EXPERTS), 0.01)
    inp["moe_w_gu"] = nrm((DEPTH, N_EXPERTS, D, 2 * D_EXPERT), D ** -0.5)
    inp["moe_b_gu"] = nrm((DEPTH, N_EXPERTS, 2 * D_EXPERT), 0.01)
    inp["moe_w_down"] = nrm((DEPTH, N_EXPERTS, D_EXPERT, D), D_EXPERT ** -0.5)
    inp["moe_b_down"] = nrm((DEPTH, N_EXPERTS, D), 0.01)
    return inp


def reference(x, c, ada_w, ada_b, ada_table, norm_mix_g, norm_ffn_g,
              lru_w_in, lru_conv_w, lru_conv_b, lru_w_a, lru_b_a, lru_w_x, lru_b_x, lru_lam, lru_w_out,
              fox_w_in, fox_b_f, fox_q_gain, fox_k_gain, fox_w_out,
              moe_w_router, moe_b_router, moe_w_gu, moe_b_gu, moe_w_down, moe_b_down):
    B = x.shape[0]
    D = x.shape[-1]
    m = jax.nn.silu(c) @ ada_w + ada_b
    for layer in range(DEPTH):
        mod = (m + ada_table[layer].reshape(-1)).reshape(B, N_MOD, D)
        h = modulate(rms_norm(x, norm_mix_g[layer]), mod[:, 0], mod[:, 1])
        if layer % N_MIXERS == 0:
            j = layer // N_MIXERS
            mix = rglru_mixer(h, lru_w_in[j], lru_conv_w[j], lru_conv_b[j], lru_w_a[j], lru_b_a[j],
                              lru_w_x[j], lru_b_x[j], lru_lam[j], lru_w_out[j])
        else:
            j = layer // N_MIXERS
            mix = fox_mixer(h, fox_w_in[j], fox_b_f[j], fox_q_gain[j], fox_k_gain[j], fox_w_out[j])
        x = x + mod[:, 2][:, None, :] * mix
        h = modulate(rms_norm(x, norm_ffn_g[layer]), mod[:, 3], mod[:, 4])
        ffn = moe_ffn(h, moe_w_router[layer], moe_b_router[layer], moe_w_gu[layer], moe_b_gu[layer],
                      moe_w_down[layer], moe_b_down[layer])
        x = x + mod[:, 5][:, None, :] * ffn
    return x
```

```python
import functools
import math

import jax
import jax.numpy as jnp
from jax import lax
from jax.experimental import pallas as pl
from jax.experimental.pallas import tpu as pltpu

F32 = jnp.float32
BF16 = jnp.bfloat16
I32 = jnp.int32

TOP_K = 4
N_MOD = 6
LRU_C = 8.0
SWIGLU_LIMIT = 7.0
SWIGLU_ALPHA = 1.702
EPS = 1e-6
LOG2E = 1.4426950408889634

LANES = 128
SUBLANES = 8
VMEM_LIMIT = 56 * 1024 * 1024
HIGHEST = lax.Precision.HIGHEST


def _params(sem, vmem=VMEM_LIMIT):
    return pltpu.CompilerParams(dimension_semantics=sem, vmem_limit_bytes=vmem)


def _tile(n, pref):
    t = min(n, pref)
    assert n % t == 0, (n, pref)
    return t


def _ada_kernel(ct_ref, w_ref, b_ref, o_ref, *, kc):
    K, nb = ct_ref.shape
    tn = w_ref.shape[1]

    def body(k, accs):
        r0 = pl.multiple_of(k * kc, kc)
        w = w_ref[pl.ds(r0, kc), :]
        c = ct_ref[pl.ds(r0, kc), :]
        s = c * jax.nn.sigmoid(c)
        return tuple(accs[b] + jnp.sum(w * s[:, b:b + 1], axis=0, keepdims=True) for b in range(nb))

    accs = lax.fori_loop(0, K // kc, body, tuple(jnp.zeros((1, tn), F32) for _ in range(nb)))
    for b in range(nb):
        o_ref[b:b + 1, :] = accs[b] + b_ref[...]


def _ada_proj(c, ada_w, ada_b):
    B, D = c.shape
    N = ada_w.shape[1]
    tn = _tile(N, 1024)
    kc = _tile(D, 256)
    return pl.pallas_call(
        functools.partial(_ada_kernel, kc=kc),
        grid=(N // tn,),
        in_specs=[pl.BlockSpec((D, B), lambda j: (0, 0)),
                  pl.BlockSpec((D, tn), lambda j: (0, j)),
                  pl.BlockSpec((1, tn), lambda j: (0, j))],
        out_specs=pl.BlockSpec((B, tn), lambda j: (0, j)),
        out_shape=jax.ShapeDtypeStruct((B, N), F32),
        compiler_params=_params(("parallel",)),
        name="ada_proj",
    )(c.T, ada_w, ada_b.reshape(1, N))


def _norm_mod_rows(x_ref, g_ref, sh_ref, sc_ref, emit, rc=64):
    tm = x_ref.shape[0]
    rc = min(rc, tm)
    gs = g_ref[...] * (1.0 + sc_ref[...])
    sh = sh_ref[...]

    def body(r, _):
        r0 = pl.multiple_of(r * rc, rc)
        x = x_ref[pl.ds(r0, rc), :]
        inv = lax.rsqrt(jnp.mean(x * x, axis=-1, keepdims=True) + EPS)
        emit(r0, rc, (x * inv) * gs + sh)
        return 0

    lax.fori_loop(0, tm // rc, body, 0)


def _norm_mm_kernel(x_ref, g_ref, sh_ref, sc_ref, w_ref, o_ref, h_ref):
    @pl.when(pl.program_id(1) == 0)
    def _():
        def emit(r0, rc, h):
            h_ref[pl.ds(r0, rc), :] = h.astype(BF16)
        _norm_mod_rows(x_ref, g_ref, sh_ref, sc_ref, emit)

    o_ref[...] = jnp.dot(h_ref[...], w_ref[...], preferred_element_type=F32).astype(o_ref.dtype)


def _norm_mm(x2, gain, shift, scale, w, S, tm=512, tn=1024):
    T, D = x2.shape
    N = w.shape[1]
    tm = _tile(S, tm)
    tn = _tile(N, tn)
    nt = S // tm
    return pl.pallas_call(
        _norm_mm_kernel,
        grid=(T // tm, N // tn),
        in_specs=[pl.BlockSpec((tm, D), lambda i, j: (i, 0)),
                  pl.BlockSpec((1, D), lambda i, j: (0, 0)),
                  pl.BlockSpec((None, 1, D), lambda i, j: (i // nt, 0, 0)),
                  pl.BlockSpec((None, 1, D), lambda i, j: (i // nt, 0, 0)),
                  pl.BlockSpec((D, tn), lambda i, j: (0, j))],
        out_specs=pl.BlockSpec((tm, tn), lambda i, j: (i, j)),
        out_shape=jax.ShapeDtypeStruct((T, N), BF16),
        scratch_shapes=[pltpu.VMEM((tm, D), BF16)],
        compiler_params=_params(("parallel", "arbitrary")),
        name="norm_mm",
    )(x2, gain, shift, scale, w)


def _lru_kernel(xb_ref, gb_ref, cw_ref, cb_ref, wax_ref, bax_ref, sp_ref, o_ref,
                xpad, a_s, b_s, hl_s, ac_s, carry):
    tt, C = xb_ref.shape
    L = tt // SUBLANES

    @pl.when(pl.program_id(2) == 0)
    def _():
        xpad[0:SUBLANES, :] = jnp.zeros((SUBLANES, C), F32)
        carry[...] = jnp.zeros_like(carry)

    xpad[SUBLANES:SUBLANES + tt, :] = xb_ref[...].astype(F32)
    cw = cw_ref[...]
    nk = cw.shape[0]
    xc = cb_ref[...] + cw[nk - 1:nk, :] * xpad[SUBLANES:SUBLANES + tt, :]
    for d in range(1, nk):
        xc = xc + cw[nk - 1 - d:nk - d, :] * xpad[SUBLANES - d:SUBLANES - d + tt, :]
    xpad[0:SUBLANES, :] = xpad[tt:tt + SUBLANES, :]

    ri = jnp.dot(xc.astype(BF16), wax_ref[...], preferred_element_type=F32) + bax_ref[...]
    r = jax.nn.sigmoid(ri[:, :C])
    ig = jax.nn.sigmoid(ri[:, C:])
    log_a = (-LRU_C) * r * sp_ref[...]
    a = jnp.exp(log_a)
    mult = jnp.sqrt(1.0 - a * a)
    nlv = C // LANES
    bt = mult * (ig * xc)
    for c in range(nlv):
        a_s[c] = a[:, c * LANES:(c + 1) * LANES]
        b_s[c] = bt[:, c * LANES:(c + 1) * LANES]

    def body(j, hA):
        j8 = pl.multiple_of(j * SUBLANES, SUBLANES)
        out = []
        for c in range(nlv):
            h, A = hA[c]
            aj = a_s[c, pl.ds(j, SUBLANES, stride=L), :]
            bj = b_s[c, pl.ds(j, SUBLANES, stride=L), :]
            h = aj * h + bj
            A = aj * A
            hl_s[c, pl.ds(j8, SUBLANES), :] = h
            ac_s[c, pl.ds(j8, SUBLANES), :] = A
            out.append((h, A))
        return tuple(out)

    init = tuple((jnp.zeros((SUBLANES, LANES), F32), jnp.ones((SUBLANES, LANES), F32)) for _ in range(nlv))
    ends = lax.fori_loop(0, L, body, init, unroll=min(L, 8))

    for c in range(nlv):
        h_end, a_end = ends[c]
        cs = slice(c * LANES, (c + 1) * LANES)
        e = carry[0:1, cs]
        for s in range(SUBLANES):
            hl = hl_s[c, pl.ds(s, L, stride=SUBLANES), :]
            ac = ac_s[c, pl.ds(s, L, stride=SUBLANES), :]
            hs = hl + ac * e
            gate = jax.nn.gelu(gb_ref[s * L:(s + 1) * L, cs].astype(F32))
            o_ref[s * L:(s + 1) * L, cs] = (hs * gate).astype(o_ref.dtype)
            e = h_end[s:s + 1, :] + a_end[s:s + 1, :] * e
        carry[0:1, cs] = e


def _lru_core(u, conv_w, conv_b, w_ax, b_ax, softplus_neg_lam, B, S, tt=1024):
    T, C2 = u.shape
    C = C2 // 2
    nh, hb, _ = w_ax.shape
    tt = _tile(S, tt)
    nt = S // tt
    return pl.pallas_call(
        _lru_kernel,
        grid=(B, nh, nt),
        in_specs=[pl.BlockSpec((tt, hb), lambda b, h, t: (b * nt + t, h)),
                  pl.BlockSpec((tt, hb), lambda b, h, t: (b * nt + t, nh + h)),
                  pl.BlockSpec((conv_w.shape[0], hb), lambda b, h, t: (0, h)),
                  pl.BlockSpec((1, hb), lambda b, h, t: (0, h)),
                  pl.BlockSpec((None, hb, 2 * hb), lambda b, h, t: (h, 0, 0)),
                  pl.BlockSpec((None, 1, 2 * hb), lambda b, h, t: (h, 0, 0)),
                  pl.BlockSpec((1, hb), lambda b, h, t: (0, h))],
        out_specs=pl.BlockSpec((tt, hb), lambda b, h, t: (b * nt + t, h)),
        out_shape=jax.ShapeDtypeStruct((T, C), BF16),
        scratch_shapes=[pltpu.VMEM((tt + SUBLANES, hb), F32),
                        pltpu.VMEM((hb // LANES, tt, LANES), F32), pltpu.VMEM((hb // LANES, tt, LANES), F32),
                        pltpu.VMEM((hb // LANES, tt, LANES), F32), pltpu.VMEM((hb // LANES, tt, LANES), F32),
                        pltpu.VMEM((SUBLANES, hb), F32)],
        compiler_params=_params(("parallel", "parallel", "arbitrary")),
        name="lru_core",
    )(u, u, conv_w, conv_b, w_ax, b_ax, softplus_neg_lam)


def _out_proj_kernel(y_ref, w_ref, x_ref, g_ref, o_ref):
    o_ref[...] = x_ref[...] + g_ref[...] * jnp.dot(y_ref[...], w_ref[...], preferred_element_type=F32)


def _out_proj(y, w, x2, gate, S, tm=1024, tn=512):
    T, D = x2.shape
    K = y.shape[1]
    tm = _tile(S, tm)
    tn = _tile(D, tn)
    nt = S // tm
    return pl.pallas_call(
        _out_proj_kernel,
        grid=(T // tm, D // tn),
        in_specs=[pl.BlockSpec((tm, K), lambda i, j: (i, 0)),
                  pl.BlockSpec((K, tn), lambda i, j: (0, j)),
                  pl.BlockSpec((tm, tn), lambda i, j: (i, j)),
                  pl.BlockSpec((None, 1, tn), lambda i, j: (i // nt, 0, j))],
        out_specs=pl.BlockSpec((tm, tn), lambda i, j: (i, j)),
        out_shape=jax.ShapeDtypeStruct((T, D), F32),
        compiler_params=_params(("parallel", "parallel")),
        name="out_proj",
    )(y, w, x2, gate)


def _router_kernel(x_ref, g_ref, sh_ref, sc_ref, wr_ref, br_ref,
                   h_ref, idx_ref, gate_ref, rank_ref, cnt_ref, carry, *, n_exp):
    tm = x_ref.shape[0]

    @pl.when(pl.program_id(0) == 0)
    def _():
        carry[...] = jnp.zeros_like(carry)

    def emit(r0, rc, h):
        h_ref[pl.ds(r0, rc), :] = h
    _norm_mod_rows(x_ref, g_ref, sh_ref, sc_ref, emit)

    logits = jnp.dot(h_ref[...], wr_ref[...], preferred_element_type=F32, precision=HIGHEST) + br_ref[...]
    v = logits.T[:n_exp, :]
    eio = lax.broadcasted_iota(I32, (n_exp, tm), 0)
    vals, idxs = [], []
    for _ in range(TOP_K):
        m = jnp.max(v, axis=0, keepdims=True)
        ik = jnp.min(jnp.where(v == m, eio, n_exp), axis=0, keepdims=True)
        vals.append(m)
        idxs.append(ik)
        v = jnp.where(eio == ik, -jnp.inf, v)
    ex = [jnp.exp(vk - vals[0]) for vk in vals]
    den = ex[0] + ex[1] + ex[2] + ex[3]

    onehot = jnp.zeros((n_exp, tm), F32)
    for ik in idxs:
        onehot = onehot + (eio == ik).astype(F32)
    si = lax.broadcasted_iota(I32, (tm, tm), 0)
    ti = lax.broadcasted_iota(I32, (tm, tm), 1)
    before = (si < ti).astype(BF16)
    base = jnp.dot(onehot.astype(BF16), before, preferred_element_type=F32) + carry[:, 0:1]

    zi = jnp.zeros((SUBLANES - TOP_K, tm), I32)
    idx_ref[...] = jnp.concatenate(idxs + [zi], axis=0)
    gate_ref[...] = jnp.concatenate([e / den for e in ex] + [zi.astype(F32)], axis=0)
    ranks = [jnp.sum(jnp.where(eio == ik, base, 0.0), axis=0, keepdims=True).astype(I32) for ik in idxs]
    rank_ref[...] = jnp.concatenate(ranks + [zi], axis=0)

    carry[...] = carry[...] + jnp.sum(onehot, axis=1, keepdims=True)
    cnt_ref[...] = carry[...]


def _router(x2, gain, shift, scale, w_r, b_r, S, tm=512):
    T, D = x2.shape
    E = w_r.shape[1]
    tm = _tile(S, tm)
    nt = S // tm
    wr = jnp.zeros((D, LANES), F32).at[:, :E].set(w_r)
    br = jnp.zeros((1, LANES), F32).at[0, :E].set(b_r)
    return pl.pallas_call(
        functools.partial(_router_kernel, n_exp=E),
        grid=(T // tm,),
        in_specs=[pl.BlockSpec((tm, D), lambda i: (i, 0)),
                  pl.BlockSpec((1, D), lambda i: (0, 0)),
                  pl.BlockSpec((None, 1, D), lambda i: (i // nt, 0, 0)),
                  pl.BlockSpec((None, 1, D), lambda i: (i // nt, 0, 0)),
                  pl.BlockSpec((D, LANES), lambda i: (0, 0)),
                  pl.BlockSpec((1, LANES), lambda i: (0, 0))],
        out_specs=[pl.BlockSpec((tm, D), lambda i: (i, 0)),
                   pl.BlockSpec((SUBLANES, tm), lambda i: (0, i)),
                   pl.BlockSpec((SUBLANES, tm), lambda i: (0, i)),
                   pl.BlockSpec((SUBLANES, tm), lambda i: (0, i)),
                   pl.BlockSpec((E, LANES), lambda i: (0, 0))],
        out_shape=[jax.ShapeDtypeStruct((T, D), F32),
                   jax.ShapeDtypeStruct((SUBLANES, T), I32),
                   jax.ShapeDtypeStruct((SUBLANES, T), F32),
                   jax.ShapeDtypeStruct((SUBLANES, T), I32),
                   jax.ShapeDtypeStruct((E, LANES), F32)],
        scratch_shapes=[pltpu.VMEM((E, LANES), F32)],
        compiler_params=_params(("arbitrary",)),
        name="moe_router",
    )(x2, gain, shift, scale, wr, br)


def _dispatch_kernel(dest_ref, h_ref, xs_in_ref, xs_ref, sem):
    del xs_in_ref
    tc = h_ref.shape[0]
    base = pl.program_id(0) * tc * TOP_K

    def copy(t, k):
        d = dest_ref[base + t * TOP_K + k]
        return pltpu.make_async_copy(h_ref.at[pl.ds(t, 1)], xs_ref.at[pl.ds(d, 1)], sem)

    def start(t, _):
        for k in range(TOP_K):
            copy(t, k).start()
        return 0

    def wait(t, _):
        for k in range(TOP_K):
            copy(t, k).wait()
        return 0

    lax.fori_loop(0, tc, start, 0)
    lax.fori_loop(0, tc, wait, 0)


def _dispatch(dest_flat, h, n_rows, tc=256):
    T, D = h.shape
    tc = _tile(T, tc)
    xs0 = jnp.zeros((n_rows, D), h.dtype)
    return pl.pallas_call(
        _dispatch_kernel,
        grid_spec=pltpu.PrefetchScalarGridSpec(
            num_scalar_prefetch=1,
            grid=(T // tc,),
            in_specs=[pl.BlockSpec((tc, D), lambda i, d: (i, 0)),
                      pl.BlockSpec(memory_space=pl.ANY)],
            out_specs=pl.BlockSpec(memory_space=pl.ANY),
            scratch_shapes=[pltpu.SemaphoreType.DMA(())]),
        out_shape=jax.ShapeDtypeStruct((n_rows, D), h.dtype),
        input_output_aliases={2: 0},
        compiler_params=_params(("arbitrary",)),
        name="moe_dispatch",
    )(dest_flat, h, xs0)


def _expert_gu_kernel(be_ref, nu_ref, xs_ref, w_ref, b_ref, o_ref):
    del be_ref
    i = pl.program_id(0)
    F = o_ref.shape[1]

    @pl.when(i < nu_ref[0])
    def _():
        gu = jnp.dot(xs_ref[...].astype(BF16), w_ref[...], preferred_element_type=F32) + b_ref[...]
        g = jnp.minimum(gu[:, :F], SWIGLU_LIMIT)
        up = jnp.clip(gu[:, F:], -SWIGLU_LIMIT, SWIGLU_LIMIT)
        o_ref[...] = ((up + 1.0) * (g * jax.nn.sigmoid(SWIGLU_ALPHA * g))).astype(o_ref.dtype)

    @pl.when(i >= nu_ref[0])
    def _():
        o_ref[...] = jnp.zeros_like(o_ref)


def _expert_down_kernel(be_ref, nu_ref, a_ref, w_ref, b_ref, o_ref):
    del be_ref
    i = pl.program_id(0)

    @pl.when(i < nu_ref[0])
    def _():
        o_ref[...] = jnp.dot(a_ref[...], w_ref[...], preferred_element_type=F32) + b_ref[...]

    @pl.when(i >= nu_ref[0])
    def _():
        o_ref[...] = jnp.zeros_like(o_ref)


def _experts(blk_expert, n_used, xs, w_gu, b_gu, w_down, b_down, tm):
    n_rows, D = xs.shape
    E, _, F2 = w_gu.shape
    F = F2 // 2
    n_blk = n_rows // tm

    def row_idx(i, be, nu):
        return (jnp.minimum(i, nu[0] - 1), 0)

    act = pl.pallas_call(
        _expert_gu_kernel,
        grid_spec=pltpu.PrefetchScalarGridSpec(
            num_scalar_prefetch=2,
            grid=(n_blk,),
            in_specs=[pl.BlockSpec((tm, D), row_idx),
                      pl.BlockSpec((None, D, F2), lambda i, be, nu: (be[i], 0, 0)),
                      pl.BlockSpec((None, 1, F2), lambda i, be, nu: (be[i], 0, 0))],
            out_specs=pl.BlockSpec((tm, F), lambda i, be, nu: (i, 0))),
        out_shape=jax.ShapeDtypeStruct((n_rows, F), BF16),
        compiler_params=_params(("arbitrary",)),
        name="moe_gate_up",
    )(blk_expert, n_used, xs, w_gu, b_gu.reshape(E, 1, F2))

    return pl.pallas_call(
        _expert_down_kernel,
        grid_spec=pltpu.PrefetchScalarGridSpec(
            num_scalar_prefetch=2,
            grid=(n_blk,),
            in_specs=[pl.BlockSpec((tm, F), row_idx),
                      pl.BlockSpec((None, F, D), lambda i, be, nu: (be[i], 0, 0)),
                      pl.BlockSpec((None, 1, D), lambda i, be, nu: (be[i], 0, 0))],
            out_specs=pl.BlockSpec((tm, D), lambda i, be, nu: (i, 0))),
        out_shape=jax.ShapeDtypeStruct((n_rows, D), F32),
        compiler_params=_params(("arbitrary",)),
        name="moe_down",
    )(blk_expert, n_used, act, w_down, b_down.reshape(E, 1, D))


def _combine_kernel(dest_ref, ys_ref, x_ref, g5_ref, gt_ref, o_ref, buf, sem):
    tc = x_ref.shape[0]
    i = pl.program_id(0)
    n = pl.num_programs(0)

    def copy(tile, slot, t, k):
        d = dest_ref[(tile * tc + t) * TOP_K + k]
        return pltpu.make_async_copy(ys_ref.at[pl.ds(d, 1)], buf.at[slot, k, pl.ds(t, 1)], sem.at[slot])

    def start_tile(tile, slot):
        def body(t, _):
            for k in range(TOP_K):
                copy(tile, slot, t, k).start()
            return 0
        lax.fori_loop(0, tc, body, 0)

    def wait_tile(tile, slot):
        def body(t, _):
            for k in range(TOP_K):
                copy(tile, slot, t, k).wait()
            return 0
        lax.fori_loop(0, tc, body, 0)

    slot = i % 2

    @pl.when(i == 0)
    def _():
        start_tile(0, 0)

    @pl.when(i + 1 < n)
    def _():
        start_tile(i + 1, 1 - slot)

    wait_tile(i, slot)
    g = gt_ref[...]
    acc = g[:, 0:1] * buf[slot, 0]
    for k in range(1, TOP_K):
        acc = acc + g[:, k:k + 1] * buf[slot, k]
    o_ref[...] = x_ref[...] + g5_ref[...] * acc


def _combine(dest_flat, ys, x2, gate5, gates_tok, S, tc=128):
    T, D = x2.shape
    tc = _tile(S, tc)
    nt = S // tc
    return pl.pallas_call(
        _combine_kernel,
        grid_spec=pltpu.PrefetchScalarGridSpec(
            num_scalar_prefetch=1,
            grid=(T // tc,),
            in_specs=[pl.BlockSpec(memory_space=pl.ANY),
                      pl.BlockSpec((tc, D), lambda i, d: (i, 0)),
                      pl.BlockSpec((None, 1, D), lambda i, d: (i // nt, 0, 0)),
                      pl.BlockSpec((tc, TOP_K), lambda i, d: (i, 0))],
            out_specs=pl.BlockSpec((tc, D), lambda i, d: (i, 0)),
            scratch_shapes=[pltpu.VMEM((2, TOP_K, tc, D), ys.dtype),
                            pltpu.SemaphoreType.DMA((2,))]),
        out_shape=jax.ShapeDtypeStruct((T, D), F32),
        compiler_params=_params(("arbitrary",)),
        name="moe_combine",
    )(dest_flat, ys, x2, gate5, gates_tok)


def _moe_layer(x2, gain, shift, scale, gate5, w_r, b_r, w_gu, b_gu, w_down, b_down, S, tm_e=256):
    T, D = x2.shape
    E = w_r.shape[1]
    h, idx, gates, rank, cnt = _router(x2, gain, shift, scale, w_r, b_r, S)
    counts = cnt[:, 0].astype(I32)
    padded = (counts + tm_e - 1) // tm_e * tm_e
    pend = jnp.cumsum(padded)
    pstart = pend - padded
    dest = (pstart[idx[:TOP_K]] + rank[:TOP_K]).T.reshape(T * TOP_K)
    n_rows = (T * TOP_K // tm_e + E) * tm_e
    n_blk = n_rows // tm_e
    n_used = (pend[E - 1] // tm_e).astype(I32)
    bi = jnp.minimum(jnp.arange(n_blk, dtype=I32), n_used - 1)
    blk_expert = jnp.minimum(jnp.searchsorted(pend, bi * tm_e, side="right"), E - 1).astype(I32)

    xs = _dispatch(dest, h, n_rows)
    ys = _experts(blk_expert, n_used.reshape(1), xs, w_gu, b_gu, w_down, b_down, tm_e)
    return _combine(dest, ys, x2, gate5, gates[:TOP_K].T, S)


def _fox_qk_kernel(x_ref, g_ref, sh_ref, sc_ref, w_ref, wf_ref, gq_ref, gk_ref, o_ref, f_ref, h_ref,
                   *, n_q_tiles, hd, q_scale):
    j = pl.program_id(1)

    @pl.when(j == 0)
    def _():
        def emit(r0, rc, h):
            h_ref[pl.ds(r0, rc), :] = h.astype(BF16)
            f_ref[pl.ds(r0, rc), :] = jnp.dot(h, wf_ref[...], preferred_element_type=F32, precision=HIGHEST)
        _norm_mod_rows(x_ref, g_ref, sh_ref, sc_ref, emit)

    y = jnp.dot(h_ref[...], w_ref[...], preferred_element_type=F32)
    gain = jnp.where(j < n_q_tiles, gq_ref[...] * q_scale, gk_ref[...])
    for hh in range(o_ref.shape[0]):
        yh = y[:, hh * hd:(hh + 1) * hd]
        inv = lax.rsqrt(jnp.mean(yh * yh, axis=-1, keepdims=True) + EPS)
        o_ref[hh] = ((yh * inv) * gain).astype(o_ref.dtype)


def _fox_v_kernel(x_ref, g_ref, sh_ref, sc_ref, w_ref, o_ref, h_ref, *, hd):
    @pl.when(pl.program_id(1) == 0)
    def _():
        def emit(r0, rc, h):
            h_ref[pl.ds(r0, rc), :] = h.astype(BF16)
        _norm_mod_rows(x_ref, g_ref, sh_ref, sc_ref, emit)

    y = jnp.dot(h_ref[...], w_ref[...], preferred_element_type=F32)
    for hh in range(o_ref.shape[0]):
        o_ref[hh] = y[:, hh * hd:(hh + 1) * hd].T.astype(o_ref.dtype)


def _fox_proj(x2, gain, shift, scale, w_qk, w_v, w_f, q_gain, k_gain, B, S, H, hd, tm, hpt=4):
    T, D = x2.shape
    tn = hpt * hd
    nt = S // tm
    n_q_tiles = H // hpt
    q_scale = (hd ** -0.5) * LOG2E
    common_specs = [pl.BlockSpec((tm, D), lambda i, j: (i, 0)),
                    pl.BlockSpec((1, D), lambda i, j: (0, 0)),
                    pl.BlockSpec((None, 1, D), lambda i, j: (i // nt, 0, 0)),
                    pl.BlockSpec((None, 1, D), lambda i, j: (i // nt, 0, 0)),
                    pl.BlockSpec((D, tn), lambda i, j: (0, j))]
    qk, f = pl.pallas_call(
        functools.partial(_fox_qk_kernel, n_q_tiles=n_q_tiles, hd=hd, q_scale=q_scale),
        grid=(T // tm, 2 * n_q_tiles),
        in_specs=common_specs + [pl.BlockSpec((D, LANES), lambda i, j: (0, 0)),
                                 pl.BlockSpec((1, hd), lambda i, j: (0, 0)),
                                 pl.BlockSpec((1, hd), lambda i, j: (0, 0))],
        out_specs=[pl.BlockSpec((None, hpt, tm, hd), lambda i, j: (i // nt, j, i % nt, 0)),
                   pl.BlockSpec((tm, LANES), lambda i, j: (i, 0))],
        out_shape=[jax.ShapeDtypeStruct((B, 2 * H, S, hd), BF16),
                   jax.ShapeDtypeStruct((T, LANES), F32)],
        scratch_shapes=[pltpu.VMEM((tm, D), BF16)],
        compiler_params=_params(("parallel", "arbitrary")),
        name="fox_qk_proj",
    )(x2, gain, shift, scale, w_qk, w_f, q_gain, k_gain)
    vt = pl.pallas_call(
        functools.partial(_fox_v_kernel, hd=hd),
        grid=(T // tm, n_q_tiles),
        in_specs=common_specs,
        out_specs=pl.BlockSpec((None, hpt, None, hd, tm), lambda i, j: (i // nt, j, i % nt, 0, 0)),
        out_shape=jax.ShapeDtypeStruct((B, H, nt, hd, tm), BF16),
        scratch_shapes=[pltpu.VMEM((tm, D), BF16)],
        compiler_params=_params(("parallel", "arbitrary")),
        name="fox_v_proj",
    )(x2, gain, shift, scale, w_v)
    return qk, vt, f


def _cum_kernel(f_ref, bf_ref, o_ref, carry):
    tt = f_ref.shape[0]

    @pl.when(pl.program_id(1) == 0)
    def _():
        carry[...] = jnp.zeros_like(carry)

    logf = jax.nn.log_sigmoid(f_ref[...] + bf_ref[...])
    si = lax.broadcasted_iota(I32, (tt, tt), 0)
    ti = lax.broadcasted_iota(I32, (tt, tt), 1)
    incl = (ti <= si).astype(F32)
    c = jnp.dot(incl, logf, preferred_element_type=F32, precision=HIGHEST) + carry[0:1, :]
    carry[0:1, :] = c[tt - 1:tt, :]
    c2 = c * LOG2E
    hi = c2.astype(BF16).astype(F32)
    r1 = c2 - hi
    mid = r1.astype(BF16).astype(F32)
    lo = (r1 - mid).astype(BF16).astype(F32)
    lane = lax.broadcasted_iota(I32, (tt, LANES), 1)
    for h in range(o_ref.shape[0]):
        t = jnp.where(lane == 0, hi[:, h:h + 1],
                      jnp.where(lane == 1, mid[:, h:h + 1],
                                jnp.where(lane == 2, lo[:, h:h + 1], 0.0)))
        o_ref[h] = t.astype(o_ref.dtype)


def _cum_forget(f, b_f, B, S, H, tt=256):
    T = f.shape[0]
    tt = _tile(S, tt)
    nt = S // tt
    bf = jnp.zeros((1, LANES), F32).at[0, :H].set(b_f)
    return pl.pallas_call(
        _cum_kernel,
        grid=(B, nt),
        in_specs=[pl.BlockSpec((tt, LANES), lambda b, t: (b * nt + t, 0)),
                  pl.BlockSpec((1, LANES), lambda b, t: (0, 0))],
        out_specs=pl.BlockSpec((None, H, tt, LANES), lambda b, t: (b, 0, t, 0)),
        out_shape=jax.ShapeDtypeStruct((B, H, S, LANES), BF16),
        scratch_shapes=[pltpu.VMEM((SUBLANES, LANES), F32)],
        compiler_params=_params(("parallel", "arbitrary")),
        name="fox_cum",
    )(f, bf)


def _fox_attn_kernel(q_ref, k_ref, c_ref, vt_ref, o_ref):
    tq, hd = q_ref.shape
    tk = vt_ref.shape[2]
    qi = pl.program_id(2)
    lane = lax.broadcasted_iota(I32, (tq, hd), 1)
    qaug = jnp.where(lane < 3, -1.0, 0.0).astype(BF16)
    qp = jnp.concatenate([q_ref[...], qaug], axis=1)

    def scores(j):
        r0 = pl.multiple_of(j * tk, tk)
        kp = jnp.concatenate([k_ref[pl.ds(r0, tk), :], c_ref[pl.ds(r0, tk), :]], axis=1)
        return lax.dot_general(kp, qp, (((1,), (1,)), ((), ())), preferred_element_type=F32)

    def update(j, t, mla):
        m, l, acc = mla
        m_new = jnp.maximum(m, jnp.max(t, axis=0, keepdims=True))
        alpha = jnp.exp2(m - m_new)
        p = jnp.exp2(t - m_new)
        l = alpha * l + jnp.sum(p, axis=0, keepdims=True)
        acc = alpha * acc + jnp.dot(vt_ref[j], p.astype(BF16), preferred_element_type=F32)
        return m_new, l, acc

    init = (jnp.full((1, tq), -jnp.inf, F32), jnp.zeros((1, tq), F32), jnp.zeros((hd, tq), F32))
    mla = lax.fori_loop(0, qi, lambda j, c: update(j, scores(j), c), init)
    t = scores(qi)
    key = lax.broadcasted_iota(I32, (tk, tq), 0)
    qry = lax.broadcasted_iota(I32, (tk, tq), 1)
    m, l, acc = update(qi, jnp.where(key <= qry, t, -jnp.inf), mla)
    o_ref[...] = (acc / l).T.astype(o_ref.dtype)


def _fox_attn(qk, caug, vt, B, S, H, hd, tq):
    nq = S // tq
    return pl.pallas_call(
        _fox_attn_kernel,
        grid=(B, H, nq),
        in_specs=[pl.BlockSpec((None, None, tq, hd), lambda b, h, q: (b, h, q, 0)),
                  pl.BlockSpec((None, None, S, hd), lambda b, h, q: (b, H + h, 0, 0)),
                  pl.BlockSpec((None, None, S, LANES), lambda b, h, q: (b, h, 0, 0)),
                  pl.BlockSpec((None, None, nq, hd, tq), lambda b, h, q: (b, h, 0, 0, 0))],
        out_specs=pl.BlockSpec((tq, hd), lambda b, h, q: (b * nq + q, h)),
        out_shape=jax.ShapeDtypeStruct((B * S, H * hd), BF16),
        compiler_params=_params(("parallel", "parallel", "arbitrary")),
        name="fox_attn",
    )(qk, qk, caug, vt)


def kernel(x, c, ada_w, ada_b, ada_table, norm_mix_g, norm_ffn_g, lru_w_in, lru_conv_w, lru_conv_b, lru_w_a, lru_b_a, lru_w_x, lru_b_x, lru_lam, lru_w_out, fox_w_in, fox_b_f, fox_q_gain, fox_k_gain, fox_w_out, moe_w_router, moe_b_router, moe_w_gu, moe_b_gu, moe_w_down, moe_b_down):
    B, S, D = x.shape
    T = B * S
    depth = ada_table.shape[0]
    H = fox_b_f.shape[1]
    hd = D // H
    assert hd == LANES and ada_table.shape[1] == N_MOD

    m = _ada_proj(c, ada_w, ada_b)
    mod = (m[None] + ada_table.reshape(depth, 1, N_MOD * D)).reshape(depth, B, N_MOD, 1, D)
    x2 = x.reshape(T, D)

    for layer in range(depth):
        shift1, scale1, gate1, shift2, scale2, gate2 = (mod[layer, :, i] for i in range(N_MOD))
        g_mix = norm_mix_g[layer].reshape(1, D)
        g_ffn = norm_ffn_g[layer].reshape(1, D)
        j = layer // 2
        if layer % 2 == 0:
            nh, hb, _ = lru_w_a[j].shape
            assert hb % LANES == 0
            u = _norm_mm(x2, g_mix, shift1, scale1, lru_w_in[j].astype(BF16), S)
            w_ax = jnp.concatenate([lru_w_a[j], lru_w_x[j]], axis=-1).astype(BF16)
            b_ax = jnp.concatenate([lru_b_a[j].reshape(nh, 1, hb), lru_b_x[j].reshape(nh, 1, hb)], axis=-1)
            sp = jax.nn.softplus(-lru_lam[j]).reshape(1, nh * hb)
            y = _lru_core(u, lru_conv_w[j], lru_conv_b[j].reshape(1, -1), w_ax, b_ax, sp, B, S)
            w_out = lru_w_out[j].astype(BF16)
        else:
            tq = _tile(S, 512)
            w_in = fox_w_in[j]
            w_qk = w_in[:, :2 * D].astype(BF16)
            w_v = w_in[:, 2 * D:3 * D].astype(BF16)
            w_f = jnp.zeros((D, LANES), F32).at[:, :H].set(w_in[:, 3 * D:])
            qk, vt, f = _fox_proj(x2, g_mix, shift1, scale1, w_qk, w_v, w_f,
                                  fox_q_gain[j].reshape(1, hd), fox_k_gain[j].reshape(1, hd), B, S, H, hd, tq)
            caug = _cum_forget(f, fox_b_f[j], B, S, H)
            y = _fox_attn(qk, caug, vt, B, S, H, hd, tq)
            w_out = fox_w_out[j].astype(BF16)
        x2 = _out_proj(y, w_out, x2, gate1, S)
        x2 = _moe_layer(x2, g_ffn, shift2, scale2, gate2, moe_w_router[layer], moe_b_router[layer],
                        moe_w_gu[layer].astype(BF16), moe_b_gu[layer], moe_w_down[layer].astype(BF16),
                        moe_b_down[layer], S)
    return x2.reshape(B, S, D)
```

```python
import functools
import math

import jax
import jax.numpy as jnp
from jax import lax
from jax.experimental import pallas as pl
from jax.experimental.pallas import tpu as pltpu

F32 = jnp.float32
BF16 = jnp.bfloat16
I32 = jnp.int32

TOP_K = 4
N_MOD = 6
LRU_C = 8.0
SWIGLU_LIMIT = 7.0
SWIGLU_ALPHA = 1.702
EPS = 1e-6
LOG2E = 1.4426950408889634

LANES = 128
SUBLANES = 8
VMEM_LIMIT = 56 * 1024 * 1024
HIGHEST = lax.Precision.HIGHEST


def _params(sem, vmem=VMEM_LIMIT):
    return pltpu.CompilerParams(dimension_semantics=sem, vmem_limit_bytes=vmem)


def _tile(n, pref):
    t = min(n, pref)
    assert n % t == 0, (n, pref)
    return t


def _ada_kernel(ct_ref, w_ref, b_ref, o_ref, *, kc):
    K, nb = ct_ref.shape
    tn = w_ref.shape[1]

    def body(k, accs):
        r0 = pl.multiple_of(k * kc, kc)
        w = w_ref[pl.ds(r0, kc), :]
        c = ct_ref[pl.ds(r0, kc), :]
        s = c * jax.nn.sigmoid(c)
        return tuple(accs[b] + jnp.sum(w * s[:, b:b + 1], axis=0, keepdims=True) for b in range(nb))

    accs = lax.fori_loop(0, K // kc, body, tuple(jnp.zeros((1, tn), F32) for _ in range(nb)))
    for b in range(nb):
        o_ref[b:b + 1, :] = accs[b] + b_ref[...]


def _ada_proj(c, ada_w, ada_b):
    B, D = c.shape
    N = ada_w.shape[1]
    tn = _tile(N, 1024)
    kc = _tile(D, 256)
    return pl.pallas_call(
        functools.partial(_ada_kernel, kc=kc),
        grid=(N // tn,),
        in_specs=[pl.BlockSpec((D, B), lambda j: (0, 0)),
                  pl.BlockSpec((D, tn), lambda j: (0, j)),
                  pl.BlockSpec((1, tn), lambda j: (0, j))],
        out_specs=pl.BlockSpec((B, tn), lambda j: (0, j)),
        out_shape=jax.ShapeDtypeStruct((B, N), F32),
        compiler_params=_params(("parallel",)),
        name="ada_proj",
    )(c.T, ada_w, ada_b.reshape(1, N))


def _norm_mod_rows(x_ref, g_ref, sh_ref, sc_ref, emit, rc=64):
    tm = x_ref.shape[0]
    rc = min(rc, tm)
    gs = g_ref[...] * (1.0 + sc_ref[...])
    sh = sh_ref[...]

    def body(r, _):
        r0 = pl.multiple_of(r * rc, rc)
        x = x_ref[pl.ds(r0, rc), :]
        inv = lax.rsqrt(jnp.mean(x * x, axis=-1, keepdims=True) + EPS)
        emit(r0, rc, (x * inv) * gs + sh)
        return 0

    lax.fori_loop(0, tm // rc, body, 0)


def _norm_mm_kernel(x_ref, g_ref, sh_ref, sc_ref, w_ref, o_ref, h_ref):
    @pl.when(pl.program_id(1) == 0)
    def _():
        def emit(r0, rc, h):
            h_ref[pl.ds(r0, rc), :] = h.astype(BF16)
        _norm_mod_rows(x_ref, g_ref, sh_ref, sc_ref, emit)

    o_ref[...] = jnp.dot(h_ref[...], w_ref[...], preferred_element_type=F32).astype(o_ref.dtype)


def _norm_mm(x2, gain, shift, scale, w, S, tm=512, tn=1024):
    T, D = x2.shape
    N = w.shape[1]
    tm = _tile(S, tm)
    tn = _tile(N, tn)
    nt = S // tm
    return pl.pallas_call(
        _norm_mm_kernel,
        grid=(T // tm, N // tn),
        in_specs=[pl.BlockSpec((tm, D), lambda i, j: (i, 0)),
                  pl.BlockSpec((1, D), lambda i, j: (0, 0)),
                  pl.BlockSpec((None, 1, D), lambda i, j: (i // nt, 0, 0)),
                  pl.BlockSpec((None, 1, D), lambda i, j: (i // nt, 0, 0)),
                  pl.BlockSpec((D, tn), lambda i, j: (0, j))],
        out_specs=pl.BlockSpec((tm, tn), lambda i, j: (i, j)),
        out_shape=jax.ShapeDtypeStruct((T, N), BF16),
        scratch_shapes=[pltpu.VMEM((tm, D), BF16)],
        compiler_params=_params(("parallel", "arbitrary")),
        name="norm_mm",
    )(x2, gain, shift, scale, w)


def _lru_kernel(xb_ref, gb_ref, cw_ref, cb_ref, wax_ref, bax_ref, sp_ref, o_ref,
                xpad, a_s, b_s, hl_s, ac_s, carry):
    tt, C = xb_ref.shape
    L = tt // SUBLANES

    @pl.when(pl.program_id(2) == 0)
    def _():
        xpad[0:SUBLANES, :] = jnp.zeros((SUBLANES, C), F32)
        carry[...] = jnp.zeros_like(carry)

    xpad[SUBLANES:SUBLANES + tt, :] = xb_ref[...].astype(F32)
    cw = cw_ref[...]
    nk = cw.shape[0]
    xc = cb_ref[...] + cw[nk - 1:nk, :] * xpad[SUBLANES:SUBLANES + tt, :]
    for d in range(1, nk):
        xc = xc + cw[nk - 1 - d:nk - d, :] * xpad[SUBLANES - d:SUBLANES - d + tt, :]
    xpad[0:SUBLANES, :] = xpad[tt:tt + SUBLANES, :]

    ri = jnp.dot(xc.astype(BF16), wax_ref[...], preferred_element_type=F32) + bax_ref[...]
    r = jax.nn.sigmoid(ri[:, :C])
    ig = jax.nn.sigmoid(ri[:, C:])
    log_a = (-LRU_C) * r * sp_ref[...]
    a = jnp.exp(log_a)
    mult = jnp.sqrt(1.0 - a * a)
    nlv = C // LANES
    bt = mult * (ig * xc)
    for c in range(nlv):
        a_s[c] = a[:, c * LANES:(c + 1) * LANES]
        b_s[c] = bt[:, c * LANES:(c + 1) * LANES]

    def body(j, hA):
        j8 = pl.multiple_of(j * SUBLANES, SUBLANES)
        out = []
        for c in range(nlv):
            h, A = hA[c]
            aj = a_s[c, pl.ds(j, SUBLANES, stride=L), :]
            bj = b_s[c, pl.ds(j, SUBLANES, stride=L), :]
            h = aj * h + bj
            A = aj * A
            hl_s[c, pl.ds(j8, SUBLANES), :] = h
            ac_s[c, pl.ds(j8, SUBLANES), :] = A
            out.append((h, A))
        return tuple(out)

    init = tuple((jnp.zeros((SUBLANES, LANES), F32), jnp.ones((SUBLANES, LANES), F32)) for _ in range(nlv))
    ends = lax.fori_loop(0, L, body, init, unroll=min(L, 8))

    for c in range(nlv):
        h_end, a_end = ends[c]
        cs = slice(c * LANES, (c + 1) * LANES)
        e = carry[0:1, cs]
        for s in range(SUBLANES):
            hl = hl_s[c, pl.ds(s, L, stride=SUBLANES), :]
            ac = ac_s[c, pl.ds(s, L, stride=SUBLANES), :]
            hs = hl + ac * e
            gate = jax.nn.gelu(gb_ref[s * L:(s + 1) * L, cs].astype(F32))
            o_ref[s * L:(s + 1) * L, cs] = (hs * gate).astype(o_ref.dtype)
            e = h_end[s:s + 1, :] + a_end[s:s + 1, :] * e
        carry[0:1, cs] = e


def _lru_core(u, conv_w, conv_b, w_ax, b_ax, softplus_neg_lam, B, S, tt=1024):
    T, C2 = u.shape
    C = C2 // 2
    nh, hb, _ = w_ax.shape
    tt = _tile(S, tt)
    nt = S // tt
    return pl.pallas_call(
        _lru_kernel,
        grid=(B, nh, nt),
        in_specs=[pl.BlockSpec((tt, hb), lambda b, h, t: (b * nt + t, h)),
                  pl.BlockSpec((tt, hb), lambda b, h, t: (b * nt + t, nh + h)),
                  pl.BlockSpec((conv_w.shape[0], hb), lambda b, h, t: (0, h)),
                  pl.BlockSpec((1, hb), lambda b, h, t: (0, h)),
                  pl.BlockSpec((None, hb, 2 * hb), lambda b, h, t: (h, 0, 0)),
                  pl.BlockSpec((None, 1, 2 * hb), lambda b, h, t: (h, 0, 0)),
                  pl.BlockSpec((1, hb), lambda b, h, t: (0, h))],
        out_specs=pl.BlockSpec((tt, hb), lambda b, h, t: (b * nt + t, h)),
        out_shape=jax.ShapeDtypeStruct((T, C), BF16),
        scratch_shapes=[pltpu.VMEM((tt + SUBLANES, hb), F32),
                        pltpu.VMEM((hb // LANES, tt, LANES), F32), pltpu.VMEM((hb // LANES, tt, LANES), F32),
                        pltpu.VMEM((hb // LANES, tt, LANES), F32), pltpu.VMEM((hb // LANES, tt, LANES), F32),
                        pltpu.VMEM((SUBLANES, hb), F32)],
        compiler_params=_params(("parallel", "parallel", "arbitrary")),
        name="lru_core",
    )(u, u, conv_w, conv_b, w_ax, b_ax, softplus_neg_lam)


def _out_proj_kernel(y_ref, w_ref, x_ref, g_ref, o_ref):
    o_ref[...] = x_ref[...] + g_ref[...] * jnp.dot(y_ref[...], w_ref[...], preferred_element_type=F32)


def _out_proj(y, w, x2, gate, S, tm=1024, tn=512):
    T, D = x2.shape
    K = y.shape[1]
    tm = _tile(S, tm)
    tn = _tile(D, tn)
    nt = S // tm
    return pl.pallas_call(
        _out_proj_kernel,
        grid=(T // tm, D // tn),
        in_specs=[pl.BlockSpec((tm, K), lambda i, j: (i, 0)),
                  pl.BlockSpec((K, tn), lambda i, j: (0, j)),
                  pl.BlockSpec((tm, tn), lambda i, j: (i, j)),
                  pl.BlockSpec((None, 1, tn), lambda i, j: (i // nt, 0, j))],
        out_specs=pl.BlockSpec((tm, tn), lambda i, j: (i, j)),
        out_shape=jax.ShapeDtypeStruct((T, D), F32),
        compiler_params=_params(("parallel", "parallel")),
        name="out_proj",
    )(y, w, x2, gate)


def _router_kernel(x_ref, g_ref, sh_ref, sc_ref, wr_ref, br_ref,
                   h_ref, idx_ref, gate_ref, rank_ref, cnt_ref, carry, *, n_exp):
    tm = x_ref.shape[0]

    @pl.when(pl.program_id(0) == 0)
    def _():
        carry[...] = jnp.zeros_like(carry)

    def emit(r0, rc, h):
        h_ref[pl.ds(r0, rc), :] = h
    _norm_mod_rows(x_ref, g_ref, sh_ref, sc_ref, emit)

    logits = jnp.dot(h_ref[...], wr_ref[...], preferred_element_type=F32, precision=HIGHEST) + br_ref[...]
    v = logits.T[:n_exp, :]
    eio = lax.broadcasted_iota(I32, (n_exp, tm), 0)
    vals, idxs = [], []
    for _ in range(TOP_K):
        m = jnp.max(v, axis=0, keepdims=True)
        ik = jnp.min(jnp.where(v == m, eio, n_exp), axis=0, keepdims=True)
        vals.append(m)
        idxs.append(ik)
        v = jnp.where(eio == ik, -jnp.inf, v)
    ex = [jnp.exp(vk - vals[0]) for vk in vals]
    den = ex[0] + ex[1] + ex[2] + ex[3]

    onehot = jnp.zeros((n_exp, tm), F32)
    for ik in idxs:
        onehot = onehot + (eio == ik).astype(F32)
    si = lax.broadcasted_iota(I32, (tm, tm), 0)
    ti = lax.broadcasted_iota(I32, (tm, tm), 1)
    before = (si < ti).astype(BF16)
    base = jnp.dot(onehot.astype(BF16), before, preferred_element_type=F32) + carry[:, 0:1]

    zi = jnp.zeros((SUBLANES - TOP_K, tm), I32)
    idx_ref[...] = jnp.concatenate(idxs + [zi], axis=0)
    gate_ref[...] = jnp.concatenate([e / den for e in ex] + [zi.astype(F32)], axis=0)
    ranks = [jnp.sum(jnp.where(eio == ik, base, 0.0), axis=0, keepdims=True).astype(I32) for ik in idxs]
    rank_ref[...] = jnp.concatenate(ranks + [zi], axis=0)

    carry[...] = carry[...] + jnp.sum(onehot, axis=1, keepdims=True)
    cnt_ref[...] = carry[...]


def _router(x2, gain, shift, scale, w_r, b_r, S, tm=512):
    T, D = x2.shape
    E = w_r.shape[1]
    tm = _tile(S, tm)
    nt = S // tm
    wr = jnp.zeros((D, LANES), F32).at[:, :E].set(w_r)
    br = jnp.zeros((1, LANES), F32).at[0, :E].set(b_r)
    return pl.pallas_call(
        functools.partial(_router_kernel, n_exp=E),
        grid=(T // tm,),
        in_specs=[pl.BlockSpec((tm, D), lambda i: (i, 0)),
                  pl.BlockSpec((1, D), lambda i: (0, 0)),
                  pl.BlockSpec((None, 1, D), lambda i: (i // nt, 0, 0)),
                  pl.BlockSpec((None, 1, D), lambda i: (i // nt, 0, 0)),
                  pl.BlockSpec((D, LANES), lambda i: (0, 0)),
                  pl.BlockSpec((1, LANES), lambda i: (0, 0))],
        out_specs=[pl.BlockSpec((tm, D), lambda i: (i, 0)),
                   pl.BlockSpec((SUBLANES, tm), lambda i: (0, i)),
                   pl.BlockSpec((SUBLANES, tm), lambda i: (0, i)),
                   pl.BlockSpec((SUBLANES, tm), lambda i: (0, i)),
                   pl.BlockSpec((E, LANES), lambda i: (0, 0))],
        out_shape=[jax.ShapeDtypeStruct((T, D), F32),
                   jax.ShapeDtypeStruct((SUBLANES, T), I32),
                   jax.ShapeDtypeStruct((SUBLANES, T), F32),
                   jax.ShapeDtypeStruct((SUBLANES, T), I32),
                   jax.ShapeDtypeStruct((E, LANES), F32)],
        scratch_shapes=[pltpu.VMEM((E, LANES), F32)],
        compiler_params=_params(("arbitrary",)),
        name="moe_router",
    )(x2, gain, shift, scale, wr, br)


def _dispatch_kernel(dest_ref, cnt_ref, pst_ref, pad_ref, nu_ref, h_ref, xs_ref, zblk, sem):
    tc = h_ref.shape[0]
    tm_e = zblk.shape[0]
    i = pl.program_id(0)
    n_tok = pl.num_programs(0) * tc
    n_exp = cnt_ref.shape[0]
    n_blk = xs_ref.shape[0] // tm_e

    def copy(t, k):
        d = dest_ref[k * n_tok + i * tc + t]
        return pltpu.make_async_copy(h_ref.at[pl.ds(t, 1)], xs_ref.at[pl.ds(d, 1)], sem.at[0])

    def pad_copy(r):
        return pltpu.make_async_copy(zblk.at[pl.ds(0, 1)], xs_ref.at[pl.ds(r, 1)], sem.at[1])

    def blk_copy(b):
        r0 = pl.multiple_of(b * tm_e, tm_e)
        return pltpu.make_async_copy(zblk, xs_ref.at[pl.ds(r0, tm_e)], sem.at[2])

    def for_pad_rows(fn):
        def per_expert(e, _):
            def per_row(r, _):
                fn(pad_copy(r))
                return 0
            lax.fori_loop(pst_ref[e] + cnt_ref[e], pst_ref[e] + pad_ref[e], per_row, 0)
            return 0
        lax.fori_loop(0, n_exp, per_expert, 0)

        def per_blk(b, _):
            fn(blk_copy(b))
            return 0
        lax.fori_loop(nu_ref[0], n_blk, per_blk, 0)

    @pl.when(i == 0)
    def _():
        zblk[...] = jnp.zeros(zblk.shape, zblk.dtype)
        for_pad_rows(lambda c: c.start())

    def start(t, _):
        for k in range(TOP_K):
            copy(t, k).start()
        return 0

    def wait(t, _):
        for k in range(TOP_K):
            copy(t, k).wait()
        return 0

    lax.fori_loop(0, tc, start, 0)
    lax.fori_loop(0, tc, wait, 0)

    @pl.when(i == 0)
    def _():
        for_pad_rows(lambda c: c.wait())


def _dispatch(dest_flat, counts, pstart, padded, n_used, h, n_rows, tm_e, tc=256):
    T, D = h.shape
    tc = _tile(T, tc)
    return pl.pallas_call(
        _dispatch_kernel,
        grid_spec=pltpu.PrefetchScalarGridSpec(
            num_scalar_prefetch=5,
            grid=(T // tc,),
            in_specs=[pl.BlockSpec((tc, D), lambda i, *_: (i, 0))],
            out_specs=pl.BlockSpec(memory_space=pl.ANY),
            scratch_shapes=[pltpu.VMEM((tm_e, D), h.dtype), pltpu.SemaphoreType.DMA((3,))]),
        out_shape=jax.ShapeDtypeStruct((n_rows, D), h.dtype),
        compiler_params=_params(("arbitrary",)),
        name="moe_dispatch",
    )(dest_flat, counts, pstart, padded, n_used, h)


def _refresh_expert_weight(be_ref, w_ref, wbf_ref, rows=256):
    i = pl.program_id(0)
    changed = (i == 0) | (be_ref[i] != be_ref[jnp.maximum(i - 1, 0)])

    @pl.when(changed)
    def _():
        rc = min(rows, w_ref.shape[0])

        def body(r, _):
            r0 = pl.multiple_of(r * rc, rc)
            wbf_ref[pl.ds(r0, rc), :] = w_ref[pl.ds(r0, rc), :].astype(BF16)
            return 0

        lax.fori_loop(0, w_ref.shape[0] // rc, body, 0)


def _expert_gate_kernel(be_ref, nu_ref, xs_ref, w_ref, b_ref, o_ref, wbf_ref):
    _refresh_expert_weight(be_ref, w_ref, wbf_ref)
    i = pl.program_id(0)

    @pl.when(i < nu_ref[0])
    def _():
        g = jnp.dot(xs_ref[...].astype(BF16), wbf_ref[...], preferred_element_type=F32) + b_ref[...]
        g = jnp.minimum(g, SWIGLU_LIMIT)
        o_ref[...] = (g * jax.nn.sigmoid(SWIGLU_ALPHA * g)).astype(o_ref.dtype)

    @pl.when(i >= nu_ref[0])
    def _():
        o_ref[...] = jnp.zeros_like(o_ref)


def _expert_up_kernel(be_ref, nu_ref, xs_ref, w_ref, b_ref, glu_ref, o_ref, wbf_ref):
    _refresh_expert_weight(be_ref, w_ref, wbf_ref)
    i = pl.program_id(0)

    @pl.when(i < nu_ref[0])
    def _():
        up = jnp.dot(xs_ref[...].astype(BF16), wbf_ref[...], preferred_element_type=F32) + b_ref[...]
        up = jnp.clip(up, -SWIGLU_LIMIT, SWIGLU_LIMIT)
        o_ref[...] = ((up + 1.0) * glu_ref[...].astype(F32)).astype(o_ref.dtype)

    @pl.when(i >= nu_ref[0])
    def _():
        o_ref[...] = jnp.zeros_like(o_ref)


def _expert_down_kernel(be_ref, nu_ref, a_ref, w_ref, b_ref, o_ref, wbf_ref):
    _refresh_expert_weight(be_ref, w_ref, wbf_ref)
    i = pl.program_id(0)

    @pl.when(i < nu_ref[0])
    def _():
        o_ref[...] = jnp.dot(a_ref[...], wbf_ref[...], preferred_element_type=F32) + b_ref[...]

    @pl.when(i >= nu_ref[0])
    def _():
        o_ref[...] = jnp.zeros_like(o_ref)


def _experts(blk_expert, n_used, xs, w_gu, b_gu, w_down, b_down, layer, tm):
    n_rows, D = xs.shape
    _, E, _, F2 = w_gu.shape
    F = F2 // 2
    n_blk = n_rows // tm
    b_gu3 = b_gu[layer].reshape(E, 1, F2)

    def row_idx(i, be, nu):
        return (jnp.minimum(i, nu[0] - 1), 0)

    def call(kern, name, half, extra_in, extra_specs):
        return pl.pallas_call(
            kern,
            grid_spec=pltpu.PrefetchScalarGridSpec(
                num_scalar_prefetch=2,
                grid=(n_blk,),
                in_specs=[pl.BlockSpec((tm, D), row_idx),
                          pl.BlockSpec((None, None, D, F), lambda i, be, nu: (layer, be[i], 0, half)),
                          pl.BlockSpec((None, 1, F), lambda i, be, nu: (be[i], 0, half))] + extra_specs,
                out_specs=pl.BlockSpec((tm, F), lambda i, be, nu: (i, 0)),
                scratch_shapes=[pltpu.VMEM((D, F), BF16)]),
            out_shape=jax.ShapeDtypeStruct((n_rows, F), BF16),
            compiler_params=_params(("arbitrary",)),
            name=name,
        )(blk_expert, n_used, xs, w_gu, b_gu3, *extra_in)

    glu = call(_expert_gate_kernel, "moe_gate", 0, [], [])
    act = call(_expert_up_kernel, "moe_up", 1, [glu], [pl.BlockSpec((tm, F), row_idx)])

    return pl.pallas_call(
        _expert_down_kernel,
        grid_spec=pltpu.PrefetchScalarGridSpec(
            num_scalar_prefetch=2,
            grid=(n_blk,),
            in_specs=[pl.BlockSpec((tm, F), row_idx),
                      pl.BlockSpec((None, None, F, D), lambda i, be, nu: (layer, be[i], 0, 0)),
                      pl.BlockSpec((None, 1, D), lambda i, be, nu: (be[i], 0, 0))],
            out_specs=pl.BlockSpec((tm, D), lambda i, be, nu: (i, 0)),
            scratch_shapes=[pltpu.VMEM((F, D), BF16)]),
        out_shape=jax.ShapeDtypeStruct((n_rows, D), F32),
        compiler_params=_params(("arbitrary",)),
        name="moe_down",
    )(blk_expert, n_used, act, w_down, b_down[layer].reshape(E, 1, D))


def _combine_kernel(dest_ref, ys_ref, x_ref, g5_ref, gt_ref, o_ref, buf, sem):
    tc = x_ref.shape[0]
    i = pl.program_id(0)
    n = pl.num_programs(0)
    n_tok = n * tc

    def copy(tile, slot, t, k):
        d = dest_ref[k * n_tok + tile * tc + t]
        return pltpu.make_async_copy(ys_ref.at[pl.ds(d, 1)], buf.at[slot, k, pl.ds(t, 1)], sem.at[slot])

    def start_tile(tile, slot):
        def body(t, _):
            for k in range(TOP_K):
                copy(tile, slot, t, k).start()
            return 0
        lax.fori_loop(0, tc, body, 0)

    def wait_tile(tile, slot):
        def body(t, _):
            for k in range(TOP_K):
                copy(tile, slot, t, k).wait()
            return 0
        lax.fori_loop(0, tc, body, 0)

    slot = i % 2

    @pl.when(i == 0)
    def _():
        start_tile(0, 0)

    @pl.when(i + 1 < n)
    def _():
        start_tile(i + 1, 1 - slot)

    wait_tile(i, slot)
    diag = lax.broadcasted_iota(I32, (tc, tc), 0) == lax.broadcasted_iota(I32, (tc, tc), 1)
    acc = None
    for k in range(TOP_K):
        col = jnp.sum(jnp.where(diag, gt_ref[k:k + 1, :], 0.0), axis=1, keepdims=True)
        term = col * buf[slot, k]
        acc = term if acc is None else acc + term
    o_ref[...] = x_ref[...] + g5_ref[...] * acc


def _combine(dest_flat, ys, x2, gate5, gates, S, tc=128):
    T, D = x2.shape
    tc = _tile(S, tc)
    nt = S // tc
    return pl.pallas_call(
        _combine_kernel,
        grid_spec=pltpu.PrefetchScalarGridSpec(
            num_scalar_prefetch=1,
            grid=(T // tc,),
            in_specs=[pl.BlockSpec(memory_space=pl.ANY),
                      pl.BlockSpec((tc, D), lambda i, d: (i, 0)),
                      pl.BlockSpec((None, 1, D), lambda i, d: (i // nt, 0, 0)),
                      pl.BlockSpec((SUBLANES, tc), lambda i, d: (0, i))],
            out_specs=pl.BlockSpec((tc, D), lambda i, d: (i, 0)),
            scratch_shapes=[pltpu.VMEM((2, TOP_K, tc, D), ys.dtype),
                            pltpu.SemaphoreType.DMA((2,))]),
        out_shape=jax.ShapeDtypeStruct((T, D), F32),
        compiler_params=_params(("arbitrary",)),
        name="moe_combine",
    )(dest_flat, ys, x2, gate5, gates)


def _moe_layer(x2, gain, shift, scale, gate5, w_r, b_r, w_gu, b_gu, w_down, b_down, layer, S, tm_e=256):
    T, D = x2.shape
    E = w_r.shape[1]
    h, idx, gates, rank, cnt = _router(x2, gain, shift, scale, w_r, b_r, S)
    counts = cnt[:, 0].astype(I32)
    padded = (counts + tm_e - 1) // tm_e * tm_e
    pend = jnp.cumsum(padded)
    pstart = pend - padded
    onehot = idx[:TOP_K, :, None] == jnp.arange(E, dtype=I32)
    dest = (jnp.sum(jnp.where(onehot, pstart, 0), axis=-1) + rank[:TOP_K]).reshape(TOP_K * T)
    n_rows = (T * TOP_K // tm_e + E) * tm_e
    n_blk = n_rows // tm_e
    n_used = (pend[E - 1] // tm_e).astype(I32)
    bi = jnp.minimum(jnp.arange(n_blk, dtype=I32), n_used - 1)
    blk_expert = jnp.minimum(jnp.sum(pend[None, :] <= (bi * tm_e)[:, None], axis=1), E - 1).astype(I32)

    n_used = n_used.reshape(1)
    xs = _dispatch(dest, counts, pstart, padded, n_used, h, n_rows, tm_e)
    ys = _experts(blk_expert, n_used, xs, w_gu, b_gu, w_down, b_down, layer, tm_e)
    return _combine(dest, ys, x2, gate5, gates, S)


def _fox_qkv_kernel(x_ref, g_ref, sh_ref, sc_ref, w_ref, wf_ref, gq_ref, gk_ref, ot_ref, ok_ref, f_ref, h_ref,
                    *, n_q_tiles, hd, q_scale):
    j = pl.program_id(1)
    hpt, n_sub, _, tk = ot_ref.shape

    @pl.when(j == 0)
    def _():
        def emit(r0, rc, h):
            h_ref[pl.ds(r0, rc), :] = h.astype(BF16)
            f_ref[pl.ds(r0, rc), :] = jnp.dot(h, wf_ref[...], preferred_element_type=F32, precision=HIGHEST)
        _norm_mod_rows(x_ref, g_ref, sh_ref, sc_ref, emit)

    y = jnp.dot(h_ref[...], w_ref[...], preferred_element_type=F32)

    def head_norm(yh, gain):
        inv = lax.rsqrt(jnp.mean(yh * yh, axis=-1, keepdims=True) + EPS)
        return (yh * inv) * gain

    def put_transposed(hh, yh):
        for s in range(n_sub):
            ot_ref[hh, s] = yh[s * tk:(s + 1) * tk, :].T.astype(ot_ref.dtype)

    @pl.when(j < n_q_tiles)
    def _():
        gain = gq_ref[...] * q_scale
        for hh in range(hpt):
            put_transposed(hh, head_norm(y[:, hh * hd:(hh + 1) * hd], gain))

    @pl.when((j >= n_q_tiles) & (j < 2 * n_q_tiles))
    def _():
        for hh in range(hpt):
            ok_ref[hh] = head_norm(y[:, hh * hd:(hh + 1) * hd], gk_ref[...]).astype(ok_ref.dtype)

    @pl.when(j >= 2 * n_q_tiles)
    def _():
        for hh in range(hpt):
            put_transposed(hh, y[:, hh * hd:(hh + 1) * hd])


def _fox_proj(x2, gain, shift, scale, w_in, w_f, q_gain, k_gain, B, S, H, hd, tm, tk, hpt=4):
    T, D = x2.shape
    tn = hpt * hd
    nt = S // tm
    n_sub = tm // tk
    n_q_tiles = H // hpt
    q_scale = (hd ** -0.5) * LOG2E

    nqt = n_q_tiles

    def t_idx(j):
        return jnp.where(j < nqt, j, jnp.where(j < 2 * nqt, nqt - 1, j - nqt))

    def k_idx(j):
        return jnp.clip(j - nqt, 0, nqt - 1)

    qvt, k, f = pl.pallas_call(
        functools.partial(_fox_qkv_kernel, n_q_tiles=nqt, hd=hd, q_scale=q_scale),
        grid=(T // tm, 3 * nqt),
        in_specs=[pl.BlockSpec((tm, D), lambda i, j: (i, 0)),
                  pl.BlockSpec((1, D), lambda i, j: (0, 0)),
                  pl.BlockSpec((None, 1, D), lambda i, j: (i // nt, 0, 0)),
                  pl.BlockSpec((None, 1, D), lambda i, j: (i // nt, 0, 0)),
                  pl.BlockSpec((D, tn), lambda i, j: (0, j)),
                  pl.BlockSpec((D, LANES), lambda i, j: (0, 0)),
                  pl.BlockSpec((1, hd), lambda i, j: (0, 0)),
                  pl.BlockSpec((1, hd), lambda i, j: (0, 0))],
        out_specs=[pl.BlockSpec((None, hpt, n_sub, hd, tk), lambda i, j: (i // nt, t_idx(j), i % nt, 0, 0)),
                   pl.BlockSpec((None, hpt, tm, hd), lambda i, j: (i // nt, k_idx(j), i % nt, 0)),
                   pl.BlockSpec((tm, LANES), lambda i, j: (i, 0))],
        out_shape=[jax.ShapeDtypeStruct((B, 2 * H, S // tk, hd, tk), BF16),
                   jax.ShapeDtypeStruct((B, H, S, hd), BF16),
                   jax.ShapeDtypeStruct((T, LANES), F32)],
        scratch_shapes=[pltpu.VMEM((tm, D), BF16)],
        compiler_params=_params(("parallel", "arbitrary")),
        name="fox_qkv_proj",
    )(x2, gain, shift, scale, w_in, w_f, q_gain, k_gain)
    return qvt, k, f


def _cum_kernel(f_ref, bf_ref, o_ref, carry):
    tt = f_ref.shape[0]

    @pl.when(pl.program_id(1) == 0)
    def _():
        carry[...] = jnp.zeros_like(carry)

    logf = jax.nn.log_sigmoid(f_ref[...] + bf_ref[...])
    si = lax.broadcasted_iota(I32, (tt, tt), 0)
    ti = lax.broadcasted_iota(I32, (tt, tt), 1)
    incl = (ti <= si).astype(F32)
    c = jnp.dot(incl, logf, preferred_element_type=F32, precision=HIGHEST) + carry[0:1, :]
    carry[0:1, :] = c[tt - 1:tt, :]
    c2 = c * LOG2E
    hi = c2.astype(BF16).astype(F32)
    r1 = c2 - hi
    mid = r1.astype(BF16).astype(F32)
    lo = (r1 - mid).astype(BF16).astype(F32)
    lane = lax.broadcasted_iota(I32, (tt, LANES), 1)
    for h in range(o_ref.shape[0]):
        t = jnp.where(lane == 0, hi[:, h:h + 1],
                      jnp.where(lane == 1, mid[:, h:h + 1],
                                jnp.where(lane == 2, lo[:, h:h + 1], 0.0)))
        o_ref[h] = t.astype(o_ref.dtype)


def _cum_forget(f, b_f, B, S, H, tt=256):
    T = f.shape[0]
    tt = _tile(S, tt)
    nt = S // tt
    bf = jnp.zeros((1, LANES), F32).at[0, :H].set(b_f)
    return pl.pallas_call(
        _cum_kernel,
        grid=(B, nt),
        in_specs=[pl.BlockSpec((tt, LANES), lambda b, t: (b * nt + t, 0)),
                  pl.BlockSpec((1, LANES), lambda b, t: (0, 0))],
        out_specs=pl.BlockSpec((None, H, tt, LANES), lambda b, t: (b, 0, t, 0)),
        out_shape=jax.ShapeDtypeStruct((B, H, S, LANES), BF16),
        scratch_shapes=[pltpu.VMEM((SUBLANES, LANES), F32)],
        compiler_params=_params(("parallel", "arbitrary")),
        name="fox_cum",
    )(f, bf)


def _fox_attn_kernel(qt_ref, k_ref, c_ref, vt_ref, o_ref, s_a, s_b, p_a, p_b, m_ref, l_ref, acc_ref):
    n_sub, hd, tk = qt_ref.shape
    tq = n_sub * tk
    assert n_sub == 2
    qi = pl.program_id(2)
    row = lax.broadcasted_iota(I32, (hd, tq), 0)
    qaug = jnp.where(row < 3, -1.0, 0.0).astype(BF16)
    qt = jnp.concatenate([qt_ref[s] for s in range(n_sub)], axis=1)
    qp = jnp.concatenate([qt, qaug], axis=0)

    def scores(j):
        r0 = pl.multiple_of(j * tk, tk)
        kp = jnp.concatenate([k_ref[pl.ds(r0, tk), :], c_ref[pl.ds(r0, tk), :]], axis=1)
        return jnp.dot(kp, qp, preferred_element_type=F32)

    def stage(j, s_cur, s_nxt, p_prev, p_cur, mask=None):
        if s_nxt is not None:
            s_nxt[...] = scores(j + 1)
        pv = jnp.dot(vt_ref[jnp.maximum(j - 1, 0)], p_prev[...], preferred_element_type=F32)
        t = s_cur[...]
        if mask is not None:
            t = jnp.where(mask, t, -jnp.inf)
        m_old = m_ref[...]
        m_new = jnp.maximum(m_old, jnp.max(t, axis=0, keepdims=True))
        alpha = jnp.exp2(m_old - m_new)
        p = jnp.exp2(t - m_new)
        l_ref[...] = alpha * l_ref[...] + jnp.sum(p, axis=0, keepdims=True)
        p_cur[...] = p.astype(BF16)
        acc_ref[...] = alpha * (acc_ref[...] + pv)
        m_ref[...] = m_new

    m_ref[...] = jnp.full(m_ref.shape, -jnp.inf, F32)
    l_ref[...] = jnp.zeros(l_ref.shape, F32)
    acc_ref[...] = jnp.zeros(acc_ref.shape, F32)
    p_b[...] = jnp.zeros(p_b.shape, BF16)
    s_a[...] = scores(0)

    def pair(p):
        stage(2 * p, s_a, s_b, p_b, p_a)
        stage(2 * p + 1, s_b, s_a, p_a, p_b)

    def quad(i, _):
        pair(2 * i)
        pair(2 * i + 1)
        return 0

    lax.fori_loop(0, qi // 2, quad, 0)

    @pl.when(qi % 2 == 1)
    def _():
        pair(qi - 1)

    key = lax.broadcasted_iota(I32, (tk, tq), 0)
    qry = lax.broadcasted_iota(I32, (tk, tq), 1)
    stage(2 * qi, s_a, s_b, p_b, p_a, mask=key <= qry)
    stage(2 * qi + 1, s_b, None, p_a, p_b, mask=key + tk <= qry)
    acc = acc_ref[...] + jnp.dot(vt_ref[2 * qi + 1], p_b[...], preferred_element_type=F32)
    o_ref[...] = (acc / l_ref[...]).T.astype(o_ref.dtype)


def _fox_attn(qvt, k, caug, B, S, H, hd, tq):
    nq = S // tq
    tk = qvt.shape[-1]
    nk = S // tk
    n_sub = tq // tk
    return pl.pallas_call(
        _fox_attn_kernel,
        grid=(B, H, nq),
        in_specs=[pl.BlockSpec((None, None, n_sub, hd, tk), lambda b, h, q: (b, h, q, 0, 0)),
                  pl.BlockSpec((None, None, S, hd), lambda b, h, q: (b, h, 0, 0)),
                  pl.BlockSpec((None, None, S, LANES), lambda b, h, q: (b, h, 0, 0)),
                  pl.BlockSpec((None, None, nk, hd, tk), lambda b, h, q: (b, H + h, 0, 0, 0))],
        out_specs=pl.BlockSpec((tq, hd), lambda b, h, q: (b * nq + q, h)),
        out_shape=jax.ShapeDtypeStruct((B * S, H * hd), BF16),
        scratch_shapes=[pltpu.VMEM((tk, tq), F32), pltpu.VMEM((tk, tq), F32),
                        pltpu.VMEM((tk, tq), BF16), pltpu.VMEM((tk, tq), BF16),
                        pltpu.VMEM((1, tq), F32), pltpu.VMEM((1, tq), F32), pltpu.VMEM((hd, tq), F32)],
        compiler_params=_params(("parallel", "parallel", "arbitrary")),
        name="fox_attn",
    )(qvt, k, caug, qvt)


def kernel(x, c, ada_w, ada_b, ada_table, norm_mix_g, norm_ffn_g, lru_w_in, lru_conv_w, lru_conv_b, lru_w_a, lru_b_a, lru_w_x, lru_b_x, lru_lam, lru_w_out, fox_w_in, fox_b_f, fox_q_gain, fox_k_gain, fox_w_out, moe_w_router, moe_b_router, moe_w_gu, moe_b_gu, moe_w_down, moe_b_down):
    B, S, D = x.shape
    T = B * S
    depth = ada_table.shape[0]
    H = fox_b_f.shape[1]
    hd = D // H
    assert hd == LANES and ada_table.shape[1] == N_MOD

    m = _ada_proj(c, ada_w, ada_b)
    mod = (m[None] + ada_table.reshape(depth, 1, N_MOD * D)).reshape(depth, B, N_MOD, 1, D)
    x2 = x.reshape(T, D)

    for layer in range(depth):
        shift1, scale1, gate1, shift2, scale2, gate2 = (mod[layer, :, i] for i in range(N_MOD))
        g_mix = norm_mix_g[layer].reshape(1, D)
        g_ffn = norm_ffn_g[layer].reshape(1, D)
        j = layer // 2
        if layer % 2 == 0:
            nh, hb, _ = lru_w_a[j].shape
            assert hb % LANES == 0
            u = _norm_mm(x2, g_mix, shift1, scale1, lru_w_in[j].astype(BF16), S)
            w_ax = jnp.concatenate([lru_w_a[j], lru_w_x[j]], axis=-1).astype(BF16)
            b_ax = jnp.concatenate([lru_b_a[j].reshape(nh, 1, hb), lru_b_x[j].reshape(nh, 1, hb)], axis=-1)
            sp = jax.nn.softplus(-lru_lam[j]).reshape(1, nh * hb)
            y = _lru_core(u, lru_conv_w[j], lru_conv_b[j].reshape(1, -1), w_ax, b_ax, sp, B, S)
            w_out = lru_w_out[j].astype(BF16)
        else:
            tq = _tile(S, 512)
            w_in = fox_w_in[j]
            w_f = jnp.zeros((D, LANES), F32).at[:, :H].set(w_in[:, 3 * D:])
            qvt, k, f = _fox_proj(x2, g_mix, shift1, scale1, w_in.astype(BF16), w_f,
                                  fox_q_gain[j].reshape(1, hd), fox_k_gain[j].reshape(1, hd),
                                  B, S, H, hd, tq, tq // 2)
            caug = _cum_forget(f, fox_b_f[j], B, S, H)
            y = _fox_attn(qvt, k, caug, B, S, H, hd, tq)
            w_out = fox_w_out[j].astype(BF16)
        x2 = _out_proj(y, w_out, x2, gate1, S)
        x2 = _moe_layer(x2, g_ffn, shift2, scale2, gate2, moe_w_router[layer], moe_b_router[layer],
                        moe_w_gu, moe_b_gu, moe_w_down, moe_b_down, layer, S)
    return x2.reshape(B, S, D)
```

```python
import functools
import math

import jax
import jax.numpy as jnp
from jax import lax
from jax.experimental import pallas as pl
from jax.experimental.pallas import tpu as pltpu

F32 = jnp.float32
BF16 = jnp.bfloat16
I32 = jnp.int32
U32 = jnp.uint32

TOP_K = 4
N_MOD = 6
LRU_C = 8.0
SWIGLU_LIMIT = 7.0
SWIGLU_ALPHA = 1.702
EPS = 1e-6
LOG2E = 1.4426950408889634

LANES = 128
SUBLANES = 8
VMEM_LIMIT = 56 * 1024 * 1024
HIGHEST = lax.Precision.HIGHEST


def _params(sem, vmem=VMEM_LIMIT):
    return pltpu.CompilerParams(dimension_semantics=sem, vmem_limit_bytes=vmem)


def _tile(n, pref):
    t = min(n, pref)
    assert n % t == 0, (n, pref)
    return t


def _ada_kernel(ct_ref, w_ref, b_ref, o_ref, *, kc):
    K, nb = ct_ref.shape
    tn = w_ref.shape[1]

    def body(k, accs):
        r0 = pl.multiple_of(k * kc, kc)
        w = w_ref[pl.ds(r0, kc), :]
        c = ct_ref[pl.ds(r0, kc), :]
        s = c * jax.nn.sigmoid(c)
        return tuple(accs[b] + jnp.sum(w * s[:, b:b + 1], axis=0, keepdims=True) for b in range(nb))

    accs = lax.fori_loop(0, K // kc, body, tuple(jnp.zeros((1, tn), F32) for _ in range(nb)))
    for b in range(nb):
        o_ref[b:b + 1, :] = accs[b] + b_ref[...]


def _ada_proj(c, ada_w, ada_b):
    B, D = c.shape
    N = ada_w.shape[1]
    tn = _tile(N, 1024)
    kc = _tile(D, 256)
    return pl.pallas_call(
        functools.partial(_ada_kernel, kc=kc),
        grid=(N // tn,),
        in_specs=[pl.BlockSpec((D, B), lambda j: (0, 0)),
                  pl.BlockSpec((D, tn), lambda j: (0, j)),
                  pl.BlockSpec((1, tn), lambda j: (0, j))],
        out_specs=pl.BlockSpec((B, tn), lambda j: (0, j)),
        out_shape=jax.ShapeDtypeStruct((B, N), F32),
        compiler_params=_params(("parallel",)),
        name="ada_proj",
    )(c.T, ada_w, ada_b.reshape(1, N))


def _norm_mod_rows(x_ref, g_ref, sh_ref, sc_ref, emit, rc=64):
    tm = x_ref.shape[0]
    rc = min(rc, tm)
    gs = g_ref[...] * (1.0 + sc_ref[...])
    sh = sh_ref[...]

    def body(r, _):
        r0 = pl.multiple_of(r * rc, rc)
        x = x_ref[pl.ds(r0, rc), :]
        inv = lax.rsqrt(jnp.mean(x * x, axis=-1, keepdims=True) + EPS)
        emit(r0, rc, (x * inv) * gs + sh)
        return 0

    lax.fori_loop(0, tm // rc, body, 0)


def _norm_mm_kernel(x_ref, g_ref, sh_ref, sc_ref, w_ref, o_ref, h_ref):
    @pl.when(pl.program_id(1) == 0)
    def _():
        def emit(r0, rc, h):
            h_ref[pl.ds(r0, rc), :] = h.astype(BF16)
        _norm_mod_rows(x_ref, g_ref, sh_ref, sc_ref, emit)

    o_ref[...] = jnp.dot(h_ref[...], w_ref[...], preferred_element_type=F32).astype(o_ref.dtype)


def _norm_mm(x2, gain, shift, scale, w, S, tm=512, tn=1024):
    T, D = x2.shape
    N = w.shape[1]
    tm = _tile(S, tm)
    tn = _tile(N, tn)
    nt = S // tm
    return pl.pallas_call(
        _norm_mm_kernel,
        grid=(T // tm, N // tn),
        in_specs=[pl.BlockSpec((tm, D), lambda i, j: (i, 0)),
                  pl.BlockSpec((1, D), lambda i, j: (0, 0)),
                  pl.BlockSpec((None, 1, D), lambda i, j: (i // nt, 0, 0)),
                  pl.BlockSpec((None, 1, D), lambda i, j: (i // nt, 0, 0)),
                  pl.BlockSpec((D, tn), lambda i, j: (0, j))],
        out_specs=pl.BlockSpec((tm, tn), lambda i, j: (i, j)),
        out_shape=jax.ShapeDtypeStruct((T, N), BF16),
        scratch_shapes=[pltpu.VMEM((tm, D), BF16)],
        compiler_params=_params(("parallel", "arbitrary")),
        name="norm_mm",
    )(x2, gain, shift, scale, w)


def _lru_kernel(xb_ref, gb_ref, cw_ref, cb_ref, wax_ref, bax_ref, sp_ref, o_ref,
                xpad, a_s, b_s, hl_s, ac_s, carry):
    tt, C = xb_ref.shape
    L = tt // SUBLANES

    @pl.when(pl.program_id(2) == 0)
    def _():
        xpad[0:SUBLANES, :] = jnp.zeros((SUBLANES, C), F32)
        carry[...] = jnp.zeros_like(carry)

    xpad[SUBLANES:SUBLANES + tt, :] = xb_ref[...].astype(F32)
    cw = cw_ref[...]
    nk = cw.shape[0]
    xc = cb_ref[...] + cw[nk - 1:nk, :] * xpad[SUBLANES:SUBLANES + tt, :]
    for d in range(1, nk):
        xc = xc + cw[nk - 1 - d:nk - d, :] * xpad[SUBLANES - d:SUBLANES - d + tt, :]
    xpad[0:SUBLANES, :] = xpad[tt:tt + SUBLANES, :]

    ri = jnp.dot(xc.astype(BF16), wax_ref[...], preferred_element_type=F32) + bax_ref[...]
    r = jax.nn.sigmoid(ri[:, :C])
    ig = jax.nn.sigmoid(ri[:, C:])
    log_a = (-LRU_C) * r * sp_ref[...]
    a = jnp.exp(log_a)
    mult = jnp.sqrt(1.0 - a * a)
    nlv = C // LANES
    bt = mult * (ig * xc)
    for c in range(nlv):
        a_s[c] = a[:, c * LANES:(c + 1) * LANES]
        b_s[c] = bt[:, c * LANES:(c + 1) * LANES]

    def body(j, hA):
        j8 = pl.multiple_of(j * SUBLANES, SUBLANES)
        out = []
        for c in range(nlv):
            h, A = hA[c]
            aj = a_s[c, pl.ds(j, SUBLANES, stride=L), :]
            bj = b_s[c, pl.ds(j, SUBLANES, stride=L), :]
            h = aj * h + bj
            A = aj * A
            hl_s[c, pl.ds(j8, SUBLANES), :] = h
            ac_s[c, pl.ds(j8, SUBLANES), :] = A
            out.append((h, A))
        return tuple(out)

    init = tuple((jnp.zeros((SUBLANES, LANES), F32), jnp.ones((SUBLANES, LANES), F32)) for _ in range(nlv))
    ends = lax.fori_loop(0, L, body, init, unroll=min(L, 8))

    for c in range(nlv):
        h_end, a_end = ends[c]
        cs = slice(c * LANES, (c + 1) * LANES)
        e = carry[0:1, cs]
        for s in range(SUBLANES):
            hl = hl_s[c, pl.ds(s, L, stride=SUBLANES), :]
            ac = ac_s[c, pl.ds(s, L, stride=SUBLANES), :]
            hs = hl + ac * e
            gate = jax.nn.gelu(gb_ref[s * L:(s + 1) * L, cs].astype(F32))
            o_ref[s * L:(s + 1) * L, cs] = (hs * gate).astype(o_ref.dtype)
            e = h_end[s:s + 1, :] + a_end[s:s + 1, :] * e
        carry[0:1, cs] = e


def _lru_core(u, conv_w, conv_b, w_ax, b_ax, softplus_neg_lam, B, S, tt=1024):
    T, C2 = u.shape
    C = C2 // 2
    nh, hb, _ = w_ax.shape
    tt = _tile(S, tt)
    nt = S // tt
    return pl.pallas_call(
        _lru_kernel,
        grid=(B, nh, nt),
        in_specs=[pl.BlockSpec((tt, hb), lambda b, h, t: (b * nt + t, h)),
                  pl.BlockSpec((tt, hb), lambda b, h, t: (b * nt + t, nh + h)),
                  pl.BlockSpec((conv_w.shape[0], hb), lambda b, h, t: (0, h)),
                  pl.BlockSpec((1, hb), lambda b, h, t: (0, h)),
                  pl.BlockSpec((None, hb, 2 * hb), lambda b, h, t: (h, 0, 0)),
                  pl.BlockSpec((None, 1, 2 * hb), lambda b, h, t: (h, 0, 0)),
                  pl.BlockSpec((1, hb), lambda b, h, t: (0, h))],
        out_specs=pl.BlockSpec((tt, hb), lambda b, h, t: (b * nt + t, h)),
        out_shape=jax.ShapeDtypeStruct((T, C), BF16),
        scratch_shapes=[pltpu.VMEM((tt + SUBLANES, hb), F32),
                        pltpu.VMEM((hb // LANES, tt, LANES), F32), pltpu.VMEM((hb // LANES, tt, LANES), F32),
                        pltpu.VMEM((hb // LANES, tt, LANES), F32), pltpu.VMEM((hb // LANES, tt, LANES), F32),
                        pltpu.VMEM((SUBLANES, hb), F32)],
        compiler_params=_params(("parallel", "parallel", "arbitrary")),
        name="lru_core",
    )(u, u, conv_w, conv_b, w_ax, b_ax, softplus_neg_lam)


def _out_proj_kernel(y_ref, w_ref, x_ref, g_ref, o_ref):
    o_ref[...] = x_ref[...] + g_ref[...] * jnp.dot(y_ref[...], w_ref[...], preferred_element_type=F32)


def _out_proj(y, w, x2, gate, S, tm=1024, tn=512):
    T, D = x2.shape
    K = y.shape[1]
    tm = _tile(S, tm)
    tn = _tile(D, tn)
    nt = S // tm
    return pl.pallas_call(
        _out_proj_kernel,
        grid=(T // tm, D // tn),
        in_specs=[pl.BlockSpec((tm, K), lambda i, j: (i, 0)),
                  pl.BlockSpec((K, tn), lambda i, j: (0, j)),
                  pl.BlockSpec((tm, tn), lambda i, j: (i, j)),
                  pl.BlockSpec((None, 1, tn), lambda i, j: (i // nt, 0, j))],
        out_specs=pl.BlockSpec((tm, tn), lambda i, j: (i, j)),
        out_shape=jax.ShapeDtypeStruct((T, D), F32),
        compiler_params=_params(("parallel", "parallel")),
        name="out_proj",
    )(y, w, x2, gate)


def _pack_bf16_pair(lo, hi):
    ul = lax.bitcast_convert_type(lo, U32)
    uh = lax.bitcast_convert_type(hi, U32)
    rl = (ul + (((ul >> 16) & 1) + 0x7FFF)) >> 16
    rh = ((uh + (((uh >> 16) & 1) + 0x7FFF)) >> 16) << 16
    return rh | rl


def _unpack_bf16_pair(w):
    lo = lax.bitcast_convert_type(w << 16, F32)
    hi = lax.bitcast_convert_type((w >> 16) << 16, F32)
    return lo, hi


def _unpack_rows_bf16(w):
    lo, hi = _unpack_bf16_pair(w)
    return jnp.concatenate([lo.astype(BF16), hi.astype(BF16)], axis=1)


def _router_kernel(x_ref, g_ref, sh_ref, sc_ref, wr_ref, br_ref,
                   hp_ref, idx_ref, gate_ref, rank_ref, cnt_ref, carry, lg_ref, *, n_exp):
    tm, D = x_ref.shape

    @pl.when(pl.program_id(0) == 0)
    def _():
        carry[...] = jnp.zeros_like(carry)

    def emit(r0, rc, h):
        hp_ref[pl.ds(r0, rc), :] = _pack_bf16_pair(h[:, :D // 2], h[:, D // 2:])
        lg_ref[pl.ds(r0, rc), :] = jnp.dot(h.astype(BF16), wr_ref[...], preferred_element_type=F32)
    _norm_mod_rows(x_ref, g_ref, sh_ref, sc_ref, emit)

    logits = lg_ref[...] + br_ref[...]
    v = logits.T[:n_exp, :]
    eio = lax.broadcasted_iota(I32, (n_exp, tm), 0)
    vals, idxs = [], []
    for _ in range(TOP_K):
        m = jnp.max(v, axis=0, keepdims=True)
        ik = jnp.min(jnp.where(v == m, eio, n_exp), axis=0, keepdims=True)
        vals.append(m)
        idxs.append(ik)
        v = jnp.where(eio == ik, -jnp.inf, v)
    ex = [jnp.exp(vk - vals[0]) for vk in vals]
    den = ex[0] + ex[1] + ex[2] + ex[3]

    onehot = jnp.zeros((n_exp, tm), F32)
    for ik in idxs:
        onehot = onehot + (eio == ik).astype(F32)
    si = lax.broadcasted_iota(I32, (tm, tm), 0)
    ti = lax.broadcasted_iota(I32, (tm, tm), 1)
    before = (si < ti).astype(BF16)
    base = jnp.dot(onehot.astype(BF16), before, preferred_element_type=F32) + carry[:, 0:1]

    zi = jnp.zeros((SUBLANES - TOP_K, tm), I32)
    idx_ref[...] = jnp.concatenate(idxs + [zi], axis=0)
    gate_ref[...] = jnp.concatenate([e / den for e in ex] + [zi.astype(F32)], axis=0)
    ranks = [jnp.sum(jnp.where(eio == ik, base, 0.0), axis=0, keepdims=True).astype(I32) for ik in idxs]
    rank_ref[...] = jnp.concatenate(ranks + [zi], axis=0)

    carry[...] = carry[...] + jnp.sum(onehot, axis=1, keepdims=True)
    cnt_ref[...] = carry[...]


def _router(x2, gain, shift, scale, w_r, b_r, S, tm=512):
    T, D = x2.shape
    E = w_r.shape[1]
    tm = _tile(S, tm)
    nt = S // tm
    wr = jnp.zeros((D, LANES), BF16).at[:, :E].set(w_r.astype(BF16))
    br = jnp.zeros((1, LANES), F32).at[0, :E].set(b_r)
    return pl.pallas_call(
        functools.partial(_router_kernel, n_exp=E),
        grid=(T // tm,),
        in_specs=[pl.BlockSpec((tm, D), lambda i: (i, 0)),
                  pl.BlockSpec((1, D), lambda i: (0, 0)),
                  pl.BlockSpec((None, 1, D), lambda i: (i // nt, 0, 0)),
                  pl.BlockSpec((None, 1, D), lambda i: (i // nt, 0, 0)),
                  pl.BlockSpec((D, LANES), lambda i: (0, 0)),
                  pl.BlockSpec((1, LANES), lambda i: (0, 0))],
        out_specs=[pl.BlockSpec((tm, D // 2), lambda i: (i, 0)),
                   pl.BlockSpec((SUBLANES, tm), lambda i: (0, i)),
                   pl.BlockSpec((SUBLANES, tm), lambda i: (0, i)),
                   pl.BlockSpec((SUBLANES, tm), lambda i: (0, i)),
                   pl.BlockSpec((E, LANES), lambda i: (0, 0))],
        out_shape=[jax.ShapeDtypeStruct((T, D // 2), U32),
                   jax.ShapeDtypeStruct((SUBLANES, T), I32),
                   jax.ShapeDtypeStruct((SUBLANES, T), F32),
                   jax.ShapeDtypeStruct((SUBLANES, T), I32),
                   jax.ShapeDtypeStruct((E, LANES), F32)],
        scratch_shapes=[pltpu.VMEM((E, LANES), F32), pltpu.VMEM((tm, LANES), F32)],
        compiler_params=_params(("arbitrary",)),
        name="moe_router",
    )(x2, gain, shift, scale, wr, br)


def _dispatch_kernel(dest_ref, cnt_ref, pst_ref, pad_ref, nu_ref, h_ref, xs_ref, zblk, sem):
    tc = h_ref.shape[0]
    tm_e = zblk.shape[0]
    i = pl.program_id(0)
    n_tok = pl.num_programs(0) * tc
    n_exp = cnt_ref.shape[0]
    n_blk = xs_ref.shape[0] // tm_e

    def copy(t, k):
        d = dest_ref[k * n_tok + i * tc + t]
        return pltpu.make_async_copy(h_ref.at[pl.ds(t, 1)], xs_ref.at[pl.ds(d, 1)], sem.at[0])

    def pad_copy(r):
        return pltpu.make_async_copy(zblk.at[pl.ds(0, 1)], xs_ref.at[pl.ds(r, 1)], sem.at[1])

    def blk_copy(b):
        r0 = pl.multiple_of(b * tm_e, tm_e)
        return pltpu.make_async_copy(zblk, xs_ref.at[pl.ds(r0, tm_e)], sem.at[2])

    def for_pad_rows(fn):
        def per_expert(e, _):
            def per_row(r, _):
                fn(pad_copy(r))
                return 0
            lax.fori_loop(pst_ref[e] + cnt_ref[e], pst_ref[e] + pad_ref[e], per_row, 0)
            return 0
        lax.fori_loop(0, n_exp, per_expert, 0)

        def per_blk(b, _):
            fn(blk_copy(b))
            return 0
        lax.fori_loop(nu_ref[0], n_blk, per_blk, 0)

    @pl.when(i == 0)
    def _():
        zblk[...] = jnp.zeros(zblk.shape, zblk.dtype)
        for_pad_rows(lambda c: c.start())

    def start(t, _):
        for k in range(TOP_K):
            copy(t, k).start()
        return 0

    def wait(t, _):
        for k in range(TOP_K):
            copy(t, k).wait()
        return 0

    lax.fori_loop(0, tc, start, 0)
    lax.fori_loop(0, tc, wait, 0)

    @pl.when(i == 0)
    def _():
        for_pad_rows(lambda c: c.wait())


def _dispatch(dest_flat, counts, pstart, padded, n_used, h, n_rows, tm_e, tc=256):
    T, D = h.shape
    tc = _tile(T, tc)
    return pl.pallas_call(
        _dispatch_kernel,
        grid_spec=pltpu.PrefetchScalarGridSpec(
            num_scalar_prefetch=5,
            grid=(T // tc,),
            in_specs=[pl.BlockSpec((tc, D), lambda i, *_: (i, 0))],
            out_specs=pl.BlockSpec(memory_space=pl.ANY),
            scratch_shapes=[pltpu.VMEM((tm_e, D), h.dtype), pltpu.SemaphoreType.DMA((3,))]),
        out_shape=jax.ShapeDtypeStruct((n_rows, D), h.dtype),
        compiler_params=_params(("arbitrary",)),
        name="moe_dispatch",
    )(dest_flat, counts, pstart, padded, n_used, h)


def _refresh_expert_weight(be_ref, w_ref, wbf_ref, rows=256):
    i = pl.program_id(0)
    changed = (i == 0) | (be_ref[i] != be_ref[jnp.maximum(i - 1, 0)])

    @pl.when(changed)
    def _():
        rc = min(rows, w_ref.shape[0])

        def body(r, _):
            r0 = pl.multiple_of(r * rc, rc)
            wbf_ref[pl.ds(r0, rc), :] = w_ref[pl.ds(r0, rc), :].astype(BF16)
            return 0

        lax.fori_loop(0, w_ref.shape[0] // rc, body, 0)


def _expert_gate_kernel(be_ref, nu_ref, xs_ref, w_ref, b_ref, o_ref, wbf_ref):
    _refresh_expert_weight(be_ref, w_ref, wbf_ref)
    i = pl.program_id(0)

    @pl.when(i < nu_ref[0])
    def _():
        g = jnp.dot(_unpack_rows_bf16(xs_ref[...]), wbf_ref[...], preferred_element_type=F32) + b_ref[...]
        g = jnp.minimum(g, SWIGLU_LIMIT)
        o_ref[...] = (g * jax.nn.sigmoid(SWIGLU_ALPHA * g)).astype(o_ref.dtype)

    @pl.when(i >= nu_ref[0])
    def _():
        o_ref[...] = jnp.zeros_like(o_ref)


def _expert_up_kernel(be_ref, nu_ref, xs_ref, w_ref, b_ref, glu_ref, o_ref, wbf_ref):
    _refresh_expert_weight(be_ref, w_ref, wbf_ref)
    i = pl.program_id(0)

    @pl.when(i < nu_ref[0])
    def _():
        up = jnp.dot(_unpack_rows_bf16(xs_ref[...]), wbf_ref[...], preferred_element_type=F32) + b_ref[...]
        up = jnp.clip(up, -SWIGLU_LIMIT, SWIGLU_LIMIT)
        o_ref[...] = ((up + 1.0) * glu_ref[...].astype(F32)).astype(o_ref.dtype)

    @pl.when(i >= nu_ref[0])
    def _():
        o_ref[...] = jnp.zeros_like(o_ref)


def _expert_down_kernel(be_ref, nu_ref, a_ref, w_ref, b_ref, o_ref, wbf_ref):
    _refresh_expert_weight(be_ref, w_ref, wbf_ref)
    i = pl.program_id(0)

    @pl.when(i < nu_ref[0])
    def _():
        y = jnp.dot(a_ref[...], wbf_ref[...], preferred_element_type=F32) + b_ref[...]
        dh = y.shape[1] // 2
        o_ref[...] = _pack_bf16_pair(y[:, :dh], y[:, dh:])

    @pl.when(i >= nu_ref[0])
    def _():
        o_ref[...] = jnp.zeros_like(o_ref)


def _experts(blk_expert, n_used, xs, w_gu, b_gu, w_down, b_down, layer, tm):
    n_rows, Dh = xs.shape
    _, E, D, F2 = w_gu.shape
    F = F2 // 2
    n_blk = n_rows // tm
    b_gu3 = b_gu[layer].reshape(E, 1, F2)

    def row_idx(i, be, nu):
        return (jnp.minimum(i, nu[0] - 1), 0)

    def call(kern, name, half, extra_in, extra_specs):
        return pl.pallas_call(
            kern,
            grid_spec=pltpu.PrefetchScalarGridSpec(
                num_scalar_prefetch=2,
                grid=(n_blk,),
                in_specs=[pl.BlockSpec((tm, Dh), row_idx),
                          pl.BlockSpec((None, None, D, F), lambda i, be, nu: (layer, be[i], 0, half)),
                          pl.BlockSpec((None, 1, F), lambda i, be, nu: (be[i], 0, half))] + extra_specs,
                out_specs=pl.BlockSpec((tm, F), lambda i, be, nu: (i, 0)),
                scratch_shapes=[pltpu.VMEM((D, F), BF16)]),
            out_shape=jax.ShapeDtypeStruct((n_rows, F), BF16),
            compiler_params=_params(("arbitrary",)),
            name=name,
        )(blk_expert, n_used, xs, w_gu, b_gu3, *extra_in)

    glu = call(_expert_gate_kernel, "moe_gate", 0, [], [])
    act = call(_expert_up_kernel, "moe_up", 1, [glu], [pl.BlockSpec((tm, F), row_idx)])

    return pl.pallas_call(
        _expert_down_kernel,
        grid_spec=pltpu.PrefetchScalarGridSpec(
            num_scalar_prefetch=2,
            grid=(n_blk,),
            in_specs=[pl.BlockSpec((tm, F), row_idx),
                      pl.BlockSpec((None, None, F, D), lambda i, be, nu: (layer, be[i], 0, 0)),
                      pl.BlockSpec((None, 1, D), lambda i, be, nu: (be[i], 0, 0))],
            out_specs=pl.BlockSpec((tm, Dh), lambda i, be, nu: (i, 0)),
            scratch_shapes=[pltpu.VMEM((F, D), BF16)]),
        out_shape=jax.ShapeDtypeStruct((n_rows, Dh), U32),
        compiler_params=_params(("arbitrary",)),
        name="moe_down",
    )(blk_expert, n_used, act, w_down, b_down[layer].reshape(E, 1, D))


def _combine_kernel(dest_ref, ys_ref, x_ref, g5_ref, gt_ref, o_ref, buf, gcol, sem):
    tc, D = x_ref.shape
    dh = D // 2
    lc = dh // SUBLANES
    i = pl.program_id(0)
    n = pl.num_programs(0)
    n_tok = n * tc

    def copy(tile, slot, t, k):
        d = dest_ref[k * n_tok + tile * tc + t]
        return pltpu.make_async_copy(ys_ref.at[pl.ds(d, 1)], buf.at[slot, k, pl.ds(t, 1)], sem.at[slot])

    def start_tile(tile, slot):
        def body(t, _):
            for k in range(TOP_K):
                copy(tile, slot, t, k).start()
            return 0
        lax.fori_loop(0, tc, body, 0)

    def wait_tile(tile, slot):
        def body(t, _):
            for k in range(TOP_K):
                copy(tile, slot, t, k).wait()
            return 0
        lax.fori_loop(0, tc, body, 0)

    slot = i % 2

    @pl.when(i == 0)
    def _():
        start_tile(0, 0)

    wait_tile(i, slot)
    diag = lax.broadcasted_iota(I32, (tc, tc), 0) == lax.broadcasted_iota(I32, (tc, tc), 1)
    for k in range(TOP_K):
        col = jnp.sum(jnp.where(diag, gt_ref[k:k + 1, :], 0.0), axis=1, keepdims=True)
        gcol[k] = jnp.broadcast_to(col, (tc, LANES))

    def group(g, issue_next):
        r0 = pl.multiple_of(g * SUBLANES, SUBLANES)
        rows = pl.ds(r0, SUBLANES)
        cb = []
        for k in range(TOP_K):
            v = gcol[k, rows, :]
            cb.append(v[:, :lc] if lc <= LANES else jnp.concatenate([v] * (lc // LANES), axis=1))
        for t in range(SUBLANES):
            if issue_next:
                for k in range(TOP_K):
                    copy(i + 1, 1 - slot, r0 + t, k).start()
            c0 = t * lc
            acc_lo = acc_hi = None
            for k in range(TOP_K):
                lo, hi = _unpack_bf16_pair(buf[slot, k, rows, c0:c0 + lc])
                acc_lo = cb[k] * lo if acc_lo is None else acc_lo + cb[k] * lo
                acc_hi = cb[k] * hi if acc_hi is None else acc_hi + cb[k] * hi
            o_ref[rows, c0:c0 + lc] = x_ref[rows, c0:c0 + lc] + g5_ref[:, c0:c0 + lc] * acc_lo
            o_ref[rows, dh + c0:dh + c0 + lc] = (x_ref[rows, dh + c0:dh + c0 + lc]
                                                 + g5_ref[:, dh + c0:dh + c0 + lc] * acc_hi)
        return 0

    @pl.when(i + 1 < n)
    def _():
        lax.fori_loop(0, tc // SUBLANES, lambda g, _: group(g, True), 0)

    @pl.when(i + 1 >= n)
    def _():
        lax.fori_loop(0, tc // SUBLANES, lambda g, _: group(g, False), 0)


def _combine(dest_flat, ys, x2, gate5, gates, S, tc=128):
    T, D = x2.shape
    tc = _tile(S, tc)
    nt = S // tc
    return pl.pallas_call(
        _combine_kernel,
        grid_spec=pltpu.PrefetchScalarGridSpec(
            num_scalar_prefetch=1,
            grid=(T // tc,),
            in_specs=[pl.BlockSpec(memory_space=pl.ANY),
                      pl.BlockSpec((tc, D), lambda i, d: (i, 0)),
                      pl.BlockSpec((None, 1, D), lambda i, d: (i // nt, 0, 0)),
                      pl.BlockSpec((SUBLANES, tc), lambda i, d: (0, i))],
            out_specs=pl.BlockSpec((tc, D), lambda i, d: (i, 0)),
            scratch_shapes=[pltpu.VMEM((2, TOP_K, tc, D // 2), ys.dtype),
                            pltpu.VMEM((TOP_K, tc, LANES), F32),
                            pltpu.SemaphoreType.DMA((2,))]),
        out_shape=jax.ShapeDtypeStruct((T, D), F32),
        compiler_params=_params(("arbitrary",)),
        name="moe_combine",
    )(dest_flat, ys, x2, gate5, gates)


def _moe_layer(x2, gain, shift, scale, gate5, w_r, b_r, w_gu, b_gu, w_down, b_down, layer, S, tm_e=256):
    T, D = x2.shape
    E = w_r.shape[1]
    h, idx, gates, rank, cnt = _router(x2, gain, shift, scale, w_r, b_r, S)
    counts = cnt[:, 0].astype(I32)
    padded = (counts + tm_e - 1) // tm_e * tm_e
    pend = jnp.cumsum(padded)
    pstart = pend - padded
    onehot = idx[:TOP_K, :, None] == jnp.arange(E, dtype=I32)
    dest = (jnp.sum(jnp.where(onehot, pstart, 0), axis=-1) + rank[:TOP_K]).reshape(TOP_K * T)
    n_rows = (T * TOP_K // tm_e + E) * tm_e
    n_blk = n_rows // tm_e
    n_used = (pend[E - 1] // tm_e).astype(I32)
    bi = jnp.minimum(jnp.arange(n_blk, dtype=I32), n_used - 1)
    blk_expert = jnp.minimum(jnp.sum(pend[None, :] <= (bi * tm_e)[:, None], axis=1), E - 1).astype(I32)

    n_used = n_used.reshape(1)
    xs = _dispatch(dest, counts, pstart, padded, n_used, h, n_rows, tm_e)
    ys = _experts(blk_expert, n_used, xs, w_gu, b_gu, w_down, b_down, layer, tm_e)
    return _combine(dest, ys, x2, gate5, gates, S)


def _fox_qkv_kernel(x_ref, g_ref, sh_ref, sc_ref, w_ref, wf_ref, gq_ref, gk_ref, ot_ref, ok_ref, f_ref, h_ref,
                    *, n_q_tiles, hd, q_scale):
    j = pl.program_id(1)
    hpt, n_sub, _, tk = ot_ref.shape

    @pl.when(j == 0)
    def _():
        def emit(r0, rc, h):
            h_ref[pl.ds(r0, rc), :] = h.astype(BF16)
        _norm_mod_rows(x_ref, g_ref, sh_ref, sc_ref, emit)
        f_ref[...] = jnp.dot(h_ref[...], wf_ref[...], preferred_element_type=F32)

    def head_norm(yh, gain):
        inv = lax.rsqrt(jnp.mean(yh * yh, axis=-1, keepdims=True) + EPS)
        return (yh * inv) * gain

    def chunks(epilogue):
        for s in range(n_sub):
            y = jnp.dot(h_ref[s * tk:(s + 1) * tk, :], w_ref[...], preferred_element_type=F32)
            for hh in range(hpt):
                epilogue(hh, s, y[:, hh * hd:(hh + 1) * hd])

    @pl.when(j < n_q_tiles)
    def _():
        gain = gq_ref[...] * q_scale

        def epilogue(hh, s, yh):
            ot_ref[hh, s] = head_norm(yh, gain).T.astype(ot_ref.dtype)
        chunks(epilogue)

    @pl.when((j >= n_q_tiles) & (j < 2 * n_q_tiles))
    def _():
        def epilogue(hh, s, yh):
            ok_ref[hh, s * tk:(s + 1) * tk, :] = head_norm(yh, gk_ref[...]).astype(ok_ref.dtype)
        chunks(epilogue)

    @pl.when(j >= 2 * n_q_tiles)
    def _():
        def epilogue(hh, s, yh):
            ot_ref[hh, s] = yh.T.astype(ot_ref.dtype)
        chunks(epilogue)


def _fox_proj(x2, gain, shift, scale, w_in, w_f, q_gain, k_gain, B, S, H, hd, tm, tk, hpt=4):
    T, D = x2.shape
    tn = hpt * hd
    nt = S // tm
    n_sub = tm // tk
    n_q_tiles = H // hpt
    q_scale = (hd ** -0.5) * LOG2E

    nqt = n_q_tiles

    def t_idx(j):
        return jnp.where(j < nqt, j, jnp.where(j < 2 * nqt, nqt - 1, j - nqt))

    def k_idx(j):
        return jnp.clip(j - nqt, 0, nqt - 1)

    qvt, k, f = pl.pallas_call(
        functools.partial(_fox_qkv_kernel, n_q_tiles=nqt, hd=hd, q_scale=q_scale),
        grid=(T // tm, 3 * nqt),
        in_specs=[pl.BlockSpec((tm, D), lambda i, j: (i, 0)),
                  pl.BlockSpec((1, D), lambda i, j: (0, 0)),
                  pl.BlockSpec((None, 1, D), lambda i, j: (i // nt, 0, 0)),
                  pl.BlockSpec((None, 1, D), lambda i, j: (i // nt, 0, 0)),
                  pl.BlockSpec((D, tn), lambda i, j: (0, j)),
                  pl.BlockSpec((D, LANES), lambda i, j: (0, 0)),
                  pl.BlockSpec((1, hd), lambda i, j: (0, 0)),
                  pl.BlockSpec((1, hd), lambda i, j: (0, 0))],
        out_specs=[pl.BlockSpec((None, hpt, n_sub, hd, tk), lambda i, j: (i // nt, t_idx(j), i % nt, 0, 0)),
                   pl.BlockSpec((None, hpt, tm, hd), lambda i, j: (i // nt, k_idx(j), i % nt, 0)),
                   pl.BlockSpec((tm, LANES), lambda i, j: (i, 0))],
        out_shape=[jax.ShapeDtypeStruct((B, 2 * H, S // tk, hd, tk), BF16),
                   jax.ShapeDtypeStruct((B, H, S, hd), BF16),
                   jax.ShapeDtypeStruct((T, LANES), F32)],
        scratch_shapes=[pltpu.VMEM((tm, D), BF16)],
        compiler_params=_params(("parallel", "arbitrary")),
        name="fox_qkv_proj",
    )(x2, gain, shift, scale, w_in, w_f, q_gain, k_gain)
    return qvt, k, f


def _cum_kernel(f_ref, bf_ref, o_ref, carry):
    tt = f_ref.shape[0]

    @pl.when(pl.program_id(1) == 0)
    def _():
        carry[...] = jnp.zeros_like(carry)

    logf = jax.nn.log_sigmoid(f_ref[...] + bf_ref[...])
    si = lax.broadcasted_iota(I32, (tt, tt), 0)
    ti = lax.broadcasted_iota(I32, (tt, tt), 1)
    incl = (ti <= si).astype(F32)
    c = jnp.dot(incl, logf, preferred_element_type=F32, precision=HIGHEST) + carry[0:1, :]
    carry[0:1, :] = c[tt - 1:tt, :]
    c2 = c * LOG2E
    hi = c2.astype(BF16).astype(F32)
    r1 = c2 - hi
    mid = r1.astype(BF16).astype(F32)
    lo = (r1 - mid).astype(BF16).astype(F32)
    lane = lax.broadcasted_iota(I32, (tt, LANES), 1)
    for h in range(o_ref.shape[0]):
        t = jnp.where(lane == 0, hi[:, h:h + 1],
                      jnp.where(lane == 1, mid[:, h:h + 1],
                                jnp.where(lane == 2, lo[:, h:h + 1], 0.0)))
        o_ref[h] = t.astype(o_ref.dtype)


def _cum_forget(f, b_f, B, S, H, tt=256):
    T = f.shape[0]
    tt = _tile(S, tt)
    nt = S // tt
    bf = jnp.zeros((1, LANES), F32).at[0, :H].set(b_f)
    return pl.pallas_call(
        _cum_kernel,
        grid=(B, nt),
        in_specs=[pl.BlockSpec((tt, LANES), lambda b, t: (b * nt + t, 0)),
                  pl.BlockSpec((1, LANES), lambda b, t: (0, 0))],
        out_specs=pl.BlockSpec((None, H, tt, LANES), lambda b, t: (b, 0, t, 0)),
        out_shape=jax.ShapeDtypeStruct((B, H, S, LANES), BF16),
        scratch_shapes=[pltpu.VMEM((SUBLANES, LANES), F32)],
        compiler_params=_params(("parallel", "arbitrary")),
        name="fox_cum",
    )(f, bf)


def _fox_attn_kernel(qt_ref, k_ref, c_ref, vt_ref, o_ref, s_a, s_b, p_a, p_b, m_ref, l_ref, acc_ref):
    n_sub, hd, tk = qt_ref.shape
    tq = n_sub * tk
    assert n_sub == 2
    qi = pl.program_id(2)
    row = lax.broadcasted_iota(I32, (hd, tq), 0)
    qaug = jnp.where(row < 3, -1.0, 0.0).astype(BF16)
    qt = jnp.concatenate([qt_ref[s] for s in range(n_sub)], axis=1)
    qp = jnp.concatenate([qt, qaug], axis=0)

    def scores(j):
        r0 = pl.multiple_of(j * tk, tk)
        kp = jnp.concatenate([k_ref[pl.ds(r0, tk), :], c_ref[pl.ds(r0, tk), :]], axis=1)
        return jnp.dot(kp, qp, preferred_element_type=F32)

    def stage(j, s_cur, s_nxt, p_prev, p_cur, mask=None):
        if s_nxt is not None:
            s_nxt[...] = scores(j + 1)
        pv = jnp.dot(vt_ref[jnp.maximum(j - 1, 0)], p_prev[...], preferred_element_type=F32)
        t = s_cur[...]
        if mask is not None:
            t = jnp.where(mask, t, -jnp.inf)
        m_old = m_ref[...]
        m_new = jnp.maximum(m_old, jnp.max(t, axis=0, keepdims=True))
        alpha = jnp.exp2(m_old - m_new)
        p = jnp.exp2(t - m_new)
        l_ref[...] = alpha * l_ref[...] + jnp.sum(p, axis=0, keepdims=True)
        p_cur[...] = p.astype(BF16)
        acc_ref[...] = alpha * (acc_ref[...] + pv)
        m_ref[...] = m_new

    m_ref[...] = jnp.full(m_ref.shape, -jnp.inf, F32)
    l_ref[...] = jnp.zeros(l_ref.shape, F32)
    acc_ref[...] = jnp.zeros(acc_ref.shape, F32)
    p_b[...] = jnp.zeros(p_b.shape, BF16)
    s_a[...] = scores(0)

    def pair(p):
        stage(2 * p, s_a, s_b, p_b, p_a)
        stage(2 * p + 1, s_b, s_a, p_a, p_b)

    def quad(i, _):
        pair(2 * i)
        pair(2 * i + 1)
        return 0

    lax.fori_loop(0, qi // 2, quad, 0)

    @pl.when(qi % 2 == 1)
    def _():
        pair(qi - 1)

    key = lax.broadcasted_iota(I32, (tk, tq), 0)
    qry = lax.broadcasted_iota(I32, (tk, tq), 1)
    stage(2 * qi, s_a, s_b, p_b, p_a, mask=key <= qry)
    stage(2 * qi + 1, s_b, None, p_a, p_b, mask=key + tk <= qry)
    acc = acc_ref[...] + jnp.dot(vt_ref[2 * qi + 1], p_b[...], preferred_element_type=F32)
    o_ref[...] = (acc / l_ref[...]).T.astype(o_ref.dtype)


def _fox_attn(qvt, k, caug, B, S, H, hd, tq):
    nq = S // tq
    tk = qvt.shape[-1]
    nk = S // tk
    n_sub = tq // tk
    return pl.pallas_call(
        _fox_attn_kernel,
        grid=(B, H, nq),
        in_specs=[pl.BlockSpec((None, None, n_sub, hd, tk), lambda b, h, q: (b, h, q, 0, 0)),
                  pl.BlockSpec((None, None, S, hd), lambda b, h, q: (b, h, 0, 0)),
                  pl.BlockSpec((None, None, S, LANES), lambda b, h, q: (b, h, 0, 0)),
                  pl.BlockSpec((None, None, nk, hd, tk), lambda b, h, q: (b, H + h, 0, 0, 0))],
        out_specs=pl.BlockSpec((tq, hd), lambda b, h, q: (b * nq + q, h)),
        out_shape=jax.ShapeDtypeStruct((B * S, H * hd), BF16),
        scratch_shapes=[pltpu.VMEM((tk, tq), F32), pltpu.VMEM((tk, tq), F32),
                        pltpu.VMEM((tk, tq), BF16), pltpu.VMEM((tk, tq), BF16),
                        pltpu.VMEM((1, tq), F32), pltpu.VMEM((1, tq), F32), pltpu.VMEM((hd, tq), F32)],
        compiler_params=_params(("parallel", "parallel", "arbitrary")),
        name="fox_attn",
    )(qvt, k, caug, qvt)


def kernel(x, c, ada_w, ada_b, ada_table, norm_mix_g, norm_ffn_g, lru_w_in, lru_conv_w, lru_conv_b, lru_w_a, lru_b_a, lru_w_x, lru_b_x, lru_lam, lru_w_out, fox_w_in, fox_b_f, fox_q_gain, fox_k_gain, fox_w_out, moe_w_router, moe_b_router, moe_w_gu, moe_b_gu, moe_w_down, moe_b_down):
    B, S, D = x.shape
    T = B * S
    depth = ada_table.shape[0]
    H = fox_b_f.shape[1]
    hd = D // H
    assert hd == LANES and ada_table.shape[1] == N_MOD

    m = _ada_proj(c, ada_w, ada_b)
    mod = (m[None] + ada_table.reshape(depth, 1, N_MOD * D)).reshape(depth, B, N_MOD, 1, D)
    x2 = x.reshape(T, D)

    for layer in range(depth):
        shift1, scale1, gate1, shift2, scale2, gate2 = (mod[layer, :, i] for i in range(N_MOD))
        g_mix = norm_mix_g[layer].reshape(1, D)
        g_ffn = norm_ffn_g[layer].reshape(1, D)
        j = layer // 2
        if layer % 2 == 0:
            nh, hb, _ = lru_w_a[j].shape
            assert hb % LANES == 0
            u = _norm_mm(x2, g_mix, shift1, scale1, lru_w_in[j].astype(BF16), S)
            w_ax = jnp.concatenate([lru_w_a[j], lru_w_x[j]], axis=-1).astype(BF16)
            b_ax = jnp.concatenate([lru_b_a[j].reshape(nh, 1, hb), lru_b_x[j].reshape(nh, 1, hb)], axis=-1)
            sp = jax.nn.softplus(-lru_lam[j]).reshape(1, nh * hb)
            y = _lru_core(u, lru_conv_w[j], lru_conv_b[j].reshape(1, -1), w_ax, b_ax, sp, B, S)
            w_out = lru_w_out[j].astype(BF16)
        else:
            tq = _tile(S, 512)
            w_in = fox_w_in[j]
            w_f = jnp.zeros((D, LANES), BF16).at[:, :H].set(w_in[:, 3 * D:].astype(BF16))
            qvt, k, f = _fox_proj(x2, g_mix, shift1, scale1, w_in.astype(BF16), w_f,
                                  fox_q_gain[j].reshape(1, hd), fox_k_gain[j].reshape(1, hd),
                                  B, S, H, hd, tq, tq // 2)
            caug = _cum_forget(f, fox_b_f[j], B, S, H)
            y = _fox_attn(qvt, k, caug, B, S, H, hd, tq)
            w_out = fox_w_out[j].astype(BF16)
        x2 = _out_proj(y, w_out, x2, gate1, S)
        x2 = _moe_layer(x2, g_ffn, shift2, scale2, gate2, moe_w_router[layer], moe_b_router[layer],
                        moe_w_gu, moe_b_gu, moe_w_down, moe_b_down, layer, S)
    return x2.reshape(B, S, D)
```

```python
import functools
import math

import jax
import jax.numpy as jnp
from jax import lax
from jax.experimental import pallas as pl
from jax.experimental.pallas import tpu as pltpu

F32 = jnp.float32
BF16 = jnp.bfloat16
I32 = jnp.int32
U32 = jnp.uint32

TOP_K = 4
N_MOD = 6
LRU_C = 8.0
SWIGLU_LIMIT = 7.0
SWIGLU_ALPHA = 1.702
EPS = 1e-6
LOG2E = 1.4426950408889634

AUG_ROWS = 16
LANES = 128
SUBLANES = 8
VMEM_LIMIT = 56 * 1024 * 1024
HIGHEST = lax.Precision.HIGHEST


def _params(sem, vmem=VMEM_LIMIT):
    return pltpu.CompilerParams(dimension_semantics=sem, vmem_limit_bytes=vmem)


def _tile(n, pref):
    t = min(n, pref)
    assert n % t == 0, (n, pref)
    return t


def _ada_kernel(ct_ref, w_ref, b_ref, o_ref, *, kc):
    K, nb = ct_ref.shape
    tn = w_ref.shape[1]

    def body(k, accs):
        r0 = pl.multiple_of(k * kc, kc)
        w = w_ref[pl.ds(r0, kc), :]
        c = ct_ref[pl.ds(r0, kc), :]
        s = c * jax.nn.sigmoid(c)
        return tuple(accs[b] + jnp.sum(w * s[:, b:b + 1], axis=0, keepdims=True) for b in range(nb))

    accs = lax.fori_loop(0, K // kc, body, tuple(jnp.zeros((1, tn), F32) for _ in range(nb)))
    for b in range(nb):
        o_ref[b:b + 1, :] = accs[b] + b_ref[...]


def _ada_proj(c, ada_w, ada_b):
    B, D = c.shape
    N = ada_w.shape[1]
    tn = _tile(N, 1024)
    kc = _tile(D, 256)
    return pl.pallas_call(
        functools.partial(_ada_kernel, kc=kc),
        grid=(N // tn,),
        in_specs=[pl.BlockSpec((D, B), lambda j: (0, 0)),
                  pl.BlockSpec((D, tn), lambda j: (0, j)),
                  pl.BlockSpec((1, tn), lambda j: (0, j))],
        out_specs=pl.BlockSpec((B, tn), lambda j: (0, j)),
        out_shape=jax.ShapeDtypeStruct((B, N), F32),
        compiler_params=_params(("parallel",)),
        name="ada_proj",
    )(c.T, ada_w, ada_b.reshape(1, N))


def _norm_mod_rows(x_ref, g_ref, sh_ref, sc_ref, emit, rc=64):
    tm = x_ref.shape[0]
    rc = min(rc, tm)
    gs = g_ref[...] * (1.0 + sc_ref[...])
    sh = sh_ref[...]

    def body(r, _):
        r0 = pl.multiple_of(r * rc, rc)
        x = x_ref[pl.ds(r0, rc), :]
        inv = lax.rsqrt(jnp.mean(x * x, axis=-1, keepdims=True) + EPS)
        emit(r0, rc, (x * inv) * gs + sh)
        return 0

    lax.fori_loop(0, tm // rc, body, 0)


def _norm_mm_kernel(x_ref, g_ref, sh_ref, sc_ref, w_ref, o_ref, h_ref):
    @pl.when(pl.program_id(1) == 0)
    def _():
        def emit(r0, rc, h):
            h_ref[pl.ds(r0, rc), :] = h.astype(BF16)
        _norm_mod_rows(x_ref, g_ref, sh_ref, sc_ref, emit)

    o_ref[...] = jnp.dot(h_ref[...], w_ref[...], preferred_element_type=F32).astype(o_ref.dtype)


def _norm_mm(x2, gain, shift, scale, w, S, tm=512, tn=1024):
    T, D = x2.shape
    N = w.shape[1]
    tm = _tile(S, tm)
    tn = _tile(N, tn)
    nt = S // tm
    return pl.pallas_call(
        _norm_mm_kernel,
        grid=(T // tm, N // tn),
        in_specs=[pl.BlockSpec((tm, D), lambda i, j: (i, 0)),
                  pl.BlockSpec((1, D), lambda i, j: (0, 0)),
                  pl.BlockSpec((None, 1, D), lambda i, j: (i // nt, 0, 0)),
                  pl.BlockSpec((None, 1, D), lambda i, j: (i // nt, 0, 0)),
                  pl.BlockSpec((D, tn), lambda i, j: (0, j))],
        out_specs=pl.BlockSpec((tm, tn), lambda i, j: (i, j)),
        out_shape=jax.ShapeDtypeStruct((T, N), BF16),
        scratch_shapes=[pltpu.VMEM((tm, D), BF16)],
        compiler_params=_params(("parallel", "arbitrary")),
        name="norm_mm",
    )(x2, gain, shift, scale, w)


def _gelu_tanh(x):
    c2 = 2.0 * math.sqrt(2.0 / math.pi)
    z2 = x * (c2 + (c2 * 0.044715) * (x * x))
    return x / (1.0 + jnp.exp(-z2))


def _lru_kernel(xb_ref, gb_ref, cw_ref, cb_ref, wax_ref, bax_ref, sp_ref, o_ref,
                xpad, a_s, b_s, hl_s, ac_s, carry):
    tt, C = xb_ref.shape
    L = tt // SUBLANES

    @pl.when(pl.program_id(2) == 0)
    def _():
        xpad[0:SUBLANES, :] = jnp.zeros((SUBLANES, C), F32)
        carry[...] = jnp.zeros_like(carry)

    xpad[SUBLANES:SUBLANES + tt, :] = xb_ref[...].astype(F32)
    cw = cw_ref[...]
    nk = cw.shape[0]
    xc = cb_ref[...] + cw[nk - 1:nk, :] * xpad[SUBLANES:SUBLANES + tt, :]
    for d in range(1, nk):
        xc = xc + cw[nk - 1 - d:nk - d, :] * xpad[SUBLANES - d:SUBLANES - d + tt, :]
    xpad[0:SUBLANES, :] = xpad[tt:tt + SUBLANES, :]

    ri = jnp.dot(xc.astype(BF16), wax_ref[...], preferred_element_type=F32) + bax_ref[...]
    r = jax.nn.sigmoid(ri[:, :C])
    ig = jax.nn.sigmoid(ri[:, C:])
    log_a = (-LRU_C) * r * sp_ref[...]
    a = jnp.exp(log_a)
    om = 1.0 - a * a
    mult = jnp.where(om > 0.0, om * lax.rsqrt(om), 0.0)
    nlv = C // LANES
    bt = mult * (ig * xc)
    for c in range(nlv):
        a_s[c] = a[:, c * LANES:(c + 1) * LANES]
        b_s[c] = bt[:, c * LANES:(c + 1) * LANES]

    def body(j, hA):
        j8 = pl.multiple_of(j * SUBLANES, SUBLANES)
        out = []
        for c in range(nlv):
            h, A = hA[c]
            aj = a_s[c, pl.ds(j, SUBLANES, stride=L), :]
            bj = b_s[c, pl.ds(j, SUBLANES, stride=L), :]
            h = aj * h + bj
            A = aj * A
            hl_s[c, pl.ds(j8, SUBLANES), :] = h
            ac_s[c, pl.ds(j8, SUBLANES), :] = A
            out.append((h, A))
        return tuple(out)

    init = tuple((jnp.zeros((SUBLANES, LANES), F32), jnp.ones((SUBLANES, LANES), F32)) for _ in range(nlv))
    ends = lax.fori_loop(0, L, body, init, unroll=min(L, 8))

    for c in range(nlv):
        h_end, a_end = ends[c]
        cs = slice(c * LANES, (c + 1) * LANES)
        e = carry[0:1, cs]
        for s in range(SUBLANES):
            hl = hl_s[c, pl.ds(s, L, stride=SUBLANES), :]
            ac = ac_s[c, pl.ds(s, L, stride=SUBLANES), :]
            hs = hl + ac * e
            gate = _gelu_tanh(gb_ref[s * L:(s + 1) * L, cs].astype(F32))
            o_ref[s * L:(s + 1) * L, cs] = (hs * gate).astype(o_ref.dtype)
            e = h_end[s:s + 1, :] + a_end[s:s + 1, :] * e
        carry[0:1, cs] = e


def _lru_core(u, conv_w, conv_b, w_ax, b_ax, softplus_neg_lam, B, S, tt=1024):
    T, C2 = u.shape
    C = C2 // 2
    nh, hb, _ = w_ax.shape
    tt = _tile(S, tt)
    nt = S // tt
    return pl.pallas_call(
        _lru_kernel,
        grid=(B, nh, nt),
        in_specs=[pl.BlockSpec((tt, hb), lambda b, h, t: (b * nt + t, h)),
                  pl.BlockSpec((tt, hb), lambda b, h, t: (b * nt + t, nh + h)),
                  pl.BlockSpec((conv_w.shape[0], hb), lambda b, h, t: (0, h)),
                  pl.BlockSpec((1, hb), lambda b, h, t: (0, h)),
                  pl.BlockSpec((None, hb, 2 * hb), lambda b, h, t: (h, 0, 0)),
                  pl.BlockSpec((None, 1, 2 * hb), lambda b, h, t: (h, 0, 0)),
                  pl.BlockSpec((1, hb), lambda b, h, t: (0, h))],
        out_specs=pl.BlockSpec((tt, hb), lambda b, h, t: (b * nt + t, h)),
        out_shape=jax.ShapeDtypeStruct((T, C), BF16),
        scratch_shapes=[pltpu.VMEM((tt + SUBLANES, hb), F32),
                        pltpu.VMEM((hb // LANES, tt, LANES), F32), pltpu.VMEM((hb // LANES, tt, LANES), F32),
                        pltpu.VMEM((hb // LANES, tt, LANES), F32), pltpu.VMEM((hb // LANES, tt, LANES), F32),
                        pltpu.VMEM((SUBLANES, hb), F32)],
        compiler_params=_params(("parallel", "parallel", "arbitrary")),
        name="lru_core",
    )(u, u, conv_w, conv_b, w_ax, b_ax, softplus_neg_lam)


def _out_proj_kernel(y_ref, w_ref, x_ref, g_ref, o_ref):
    o_ref[...] = x_ref[...] + g_ref[...] * jnp.dot(y_ref[...], w_ref[...], preferred_element_type=F32)


def _out_proj(y, w, x2, gate, S, tm=1024, tn=512):
    T, D = x2.shape
    K = y.shape[1]
    tm = _tile(S, tm)
    tn = _tile(D, tn)
    nt = S // tm
    return pl.pallas_call(
        _out_proj_kernel,
        grid=(T // tm, D // tn),
        in_specs=[pl.BlockSpec((tm, K), lambda i, j: (i, 0)),
                  pl.BlockSpec((K, tn), lambda i, j: (0, j)),
                  pl.BlockSpec((tm, tn), lambda i, j: (i, j)),
                  pl.BlockSpec((None, 1, tn), lambda i, j: (i // nt, 0, j))],
        out_specs=pl.BlockSpec((tm, tn), lambda i, j: (i, j)),
        out_shape=jax.ShapeDtypeStruct((T, D), F32),
        compiler_params=_params(("parallel", "parallel")),
        name="out_proj",
    )(y, w, x2, gate)


def _pack_bf16_pair(lo, hi):
    ul = lax.bitcast_convert_type(lo, U32)
    uh = lax.bitcast_convert_type(hi, U32)
    rl = (ul + (((ul >> 16) & 1) + 0x7FFF)) >> 16
    rh = ((uh + (((uh >> 16) & 1) + 0x7FFF)) >> 16) << 16
    return rh | rl


def _unpack_bf16_pair(w):
    lo = lax.bitcast_convert_type(w << 16, F32)
    hi = lax.bitcast_convert_type((w >> 16) << 16, F32)
    return lo, hi


def _unpack_rows_bf16(w):
    lo, hi = _unpack_bf16_pair(w)
    return jnp.concatenate([lo.astype(BF16), hi.astype(BF16)], axis=1)


def _router_kernel(x_ref, g_ref, sh_ref, sc_ref, wr_ref, br_ref,
                   hp_ref, idx_ref, gate_ref, rank_ref, cnt_ref, carry, lg_ref, *, n_exp):
    tm, D = x_ref.shape

    @pl.when(pl.program_id(0) == 0)
    def _():
        carry[...] = jnp.zeros_like(carry)

    def emit(r0, rc, h):
        hp_ref[pl.ds(r0, rc), :] = _pack_bf16_pair(h[:, :D // 2], h[:, D // 2:])
        lg_ref[pl.ds(r0, rc), :] = jnp.dot(h.astype(BF16), wr_ref[...], preferred_element_type=F32)
    _norm_mod_rows(x_ref, g_ref, sh_ref, sc_ref, emit)

    logits = lg_ref[...] + br_ref[...]
    v = logits.T[:n_exp, :]
    eio = lax.broadcasted_iota(I32, (n_exp, tm), 0)
    vals, idxs = [], []
    for _ in range(TOP_K):
        m = jnp.max(v, axis=0, keepdims=True)
        ik = jnp.min(jnp.where(v == m, eio, n_exp), axis=0, keepdims=True)
        vals.append(m)
        idxs.append(ik)
        v = jnp.where(eio == ik, -jnp.inf, v)
    ex = [jnp.exp(vk - vals[0]) for vk in vals]
    den = ex[0] + ex[1] + ex[2] + ex[3]

    onehot = jnp.zeros((n_exp, tm), F32)
    for ik in idxs:
        onehot = onehot + (eio == ik).astype(F32)
    si = lax.broadcasted_iota(I32, (tm, tm), 0)
    ti = lax.broadcasted_iota(I32, (tm, tm), 1)
    before = (si < ti).astype(BF16)
    base = jnp.dot(onehot.astype(BF16), before, preferred_element_type=F32) + carry[:, 0:1]

    zi = jnp.zeros((SUBLANES - TOP_K, tm), I32)
    idx_ref[...] = jnp.concatenate(idxs + [zi], axis=0)
    gate_ref[...] = jnp.concatenate([e / den for e in ex] + [zi.astype(F32)], axis=0)
    ranks = [jnp.sum(jnp.where(eio == ik, base, 0.0), axis=0, keepdims=True).astype(I32) for ik in idxs]
    rank_ref[...] = jnp.concatenate(ranks + [zi], axis=0)

    carry[...] = carry[...] + jnp.sum(onehot, axis=1, keepdims=True)
    cnt_ref[...] = carry[...]


def _router(x2, gain, shift, scale, w_r, b_r, S, tm=512):
    T, D = x2.shape
    E = w_r.shape[1]
    tm = _tile(S, tm)
    nt = S // tm
    wr = jnp.zeros((D, LANES), BF16).at[:, :E].set(w_r.astype(BF16))
    br = jnp.zeros((1, LANES), F32).at[0, :E].set(b_r)
    return pl.pallas_call(
        functools.partial(_router_kernel, n_exp=E),
        grid=(T // tm,),
        in_specs=[pl.BlockSpec((tm, D), lambda i: (i, 0)),
                  pl.BlockSpec((1, D), lambda i: (0, 0)),
                  pl.BlockSpec((None, 1, D), lambda i: (i // nt, 0, 0)),
                  pl.BlockSpec((None, 1, D), lambda i: (i // nt, 0, 0)),
                  pl.BlockSpec((D, LANES), lambda i: (0, 0)),
                  pl.BlockSpec((1, LANES), lambda i: (0, 0))],
        out_specs=[pl.BlockSpec((tm, D // 2), lambda i: (i, 0)),
                   pl.BlockSpec((SUBLANES, tm), lambda i: (0, i)),
                   pl.BlockSpec((SUBLANES, tm), lambda i: (0, i)),
                   pl.BlockSpec((SUBLANES, tm), lambda i: (0, i)),
                   pl.BlockSpec((E, LANES), lambda i: (0, 0))],
        out_shape=[jax.ShapeDtypeStruct((T, D // 2), U32),
                   jax.ShapeDtypeStruct((SUBLANES, T), I32),
                   jax.ShapeDtypeStruct((SUBLANES, T), F32),
                   jax.ShapeDtypeStruct((SUBLANES, T), I32),
                   jax.ShapeDtypeStruct((E, LANES), F32)],
        scratch_shapes=[pltpu.VMEM((E, LANES), F32), pltpu.VMEM((tm, LANES), F32)],
        compiler_params=_params(("arbitrary",)),
        name="moe_router",
    )(x2, gain, shift, scale, wr, br)


def _dispatch_kernel(dest_ref, cnt_ref, pst_ref, pad_ref, nu_ref, h_ref, xs_ref, zblk, sem):
    tc = h_ref.shape[0]
    tm_e = zblk.shape[0]
    i = pl.program_id(0)
    n_tok = pl.num_programs(0) * tc
    n_exp = cnt_ref.shape[0]
    n_blk = xs_ref.shape[0] // tm_e

    def copy(t, k):
        d = dest_ref[k * n_tok + i * tc + t]
        return pltpu.make_async_copy(h_ref.at[pl.ds(t, 1)], xs_ref.at[pl.ds(d, 1)], sem.at[0])

    def pad_copy(r):
        return pltpu.make_async_copy(zblk.at[pl.ds(0, 1)], xs_ref.at[pl.ds(r, 1)], sem.at[1])

    def blk_copy(b):
        r0 = pl.multiple_of(b * tm_e, tm_e)
        return pltpu.make_async_copy(zblk, xs_ref.at[pl.ds(r0, tm_e)], sem.at[2])

    def for_pad_rows(fn):
        def per_expert(e, _):
            def per_row(r, _):
                fn(pad_copy(r))
                return 0
            lax.fori_loop(pst_ref[e] + cnt_ref[e], pst_ref[e] + pad_ref[e], per_row, 0)
            return 0
        lax.fori_loop(0, n_exp, per_expert, 0)

        def per_blk(b, _):
            fn(blk_copy(b))
            return 0
        lax.fori_loop(nu_ref[0], n_blk, per_blk, 0)

    @pl.when(i == 0)
    def _():
        zblk[...] = jnp.zeros(zblk.shape, zblk.dtype)
        for_pad_rows(lambda c: c.start())

    def start(t, _):
        for k in range(TOP_K):
            copy(t, k).start()
        return 0

    def wait(t, _):
        for k in range(TOP_K):
            copy(t, k).wait()
        return 0

    lax.fori_loop(0, tc, start, 0)
    lax.fori_loop(0, tc, wait, 0)

    @pl.when(i == 0)
    def _():
        for_pad_rows(lambda c: c.wait())


def _dispatch(dest_flat, counts, pstart, padded, n_used, h, n_rows, tm_e, tc=256):
    T, D = h.shape
    tc = _tile(T, tc)
    return pl.pallas_call(
        _dispatch_kernel,
        grid_spec=pltpu.PrefetchScalarGridSpec(
            num_scalar_prefetch=5,
            grid=(T // tc,),
            in_specs=[pl.BlockSpec((tc, D), lambda i, *_: (i, 0))],
            out_specs=pl.BlockSpec(memory_space=pl.ANY),
            scratch_shapes=[pltpu.VMEM((tm_e, D), h.dtype), pltpu.SemaphoreType.DMA((3,))]),
        out_shape=jax.ShapeDtypeStruct((n_rows, D), h.dtype),
        compiler_params=_params(("arbitrary",)),
        name="moe_dispatch",
    )(dest_flat, counts, pstart, padded, n_used, h)


def _refresh_expert_weight(be_ref, w_ref, wbf_ref, rows=256):
    i = pl.program_id(0)
    changed = (i == 0) | (be_ref[i] != be_ref[jnp.maximum(i - 1, 0)])

    @pl.when(changed)
    def _():
        rc = min(rows, w_ref.shape[0])

        def body(r, _):
            r0 = pl.multiple_of(r * rc, rc)
            wbf_ref[pl.ds(r0, rc), :] = w_ref[pl.ds(r0, rc), :].astype(BF16)
            return 0

        lax.fori_loop(0, w_ref.shape[0] // rc, body, 0)


def _expert_gate_kernel(be_ref, nu_ref, xs_ref, w_ref, b_ref, o_ref, wbf_ref):
    _refresh_expert_weight(be_ref, w_ref, wbf_ref)
    i = pl.program_id(0)

    @pl.when(i < nu_ref[0])
    def _():
        g = jnp.dot(_unpack_rows_bf16(xs_ref[...]), wbf_ref[...], preferred_element_type=F32) + b_ref[...]
        g = jnp.minimum(g, SWIGLU_LIMIT)
        o_ref[...] = (g * jax.nn.sigmoid(SWIGLU_ALPHA * g)).astype(o_ref.dtype)

    @pl.when(i >= nu_ref[0])
    def _():
        o_ref[...] = jnp.zeros_like(o_ref)


def _expert_up_kernel(be_ref, nu_ref, xs_ref, w_ref, b_ref, glu_ref, o_ref, wbf_ref):
    _refresh_expert_weight(be_ref, w_ref, wbf_ref)
    i = pl.program_id(0)

    @pl.when(i < nu_ref[0])
    def _():
        up = jnp.dot(_unpack_rows_bf16(xs_ref[...]), wbf_ref[...], preferred_element_type=F32) + b_ref[...]
        up = jnp.clip(up, -SWIGLU_LIMIT, SWIGLU_LIMIT)
        o_ref[...] = ((up + 1.0) * glu_ref[...].astype(F32)).astype(o_ref.dtype)

    @pl.when(i >= nu_ref[0])
    def _():
        o_ref[...] = jnp.zeros_like(o_ref)


def _expert_down_kernel(be_ref, nu_ref, a_ref, w_ref, b_ref, o_ref, wbf_ref):
    _refresh_expert_weight(be_ref, w_ref, wbf_ref)
    i = pl.program_id(0)

    @pl.when(i < nu_ref[0])
    def _():
        y = jnp.dot(a_ref[...], wbf_ref[...], preferred_element_type=F32) + b_ref[...]
        dh = y.shape[1] // 2
        o_ref[...] = _pack_bf16_pair(y[:, :dh], y[:, dh:])

    @pl.when(i >= nu_ref[0])
    def _():
        o_ref[...] = jnp.zeros_like(o_ref)


def _experts(blk_expert, n_used, xs, w_gu, b_gu, w_down, b_down, layer, tm):
    n_rows, Dh = xs.shape
    _, E, D, F2 = w_gu.shape
    F = F2 // 2
    n_blk = n_rows // tm
    b_gu3 = b_gu[layer].reshape(E, 1, F2)

    def row_idx(i, be, nu):
        return (jnp.minimum(i, nu[0] - 1), 0)

    def call(kern, name, half, extra_in, extra_specs):
        return pl.pallas_call(
            kern,
            grid_spec=pltpu.PrefetchScalarGridSpec(
                num_scalar_prefetch=2,
                grid=(n_blk,),
                in_specs=[pl.BlockSpec((tm, Dh), row_idx),
                          pl.BlockSpec((None, None, D, F), lambda i, be, nu: (layer, be[i], 0, half)),
                          pl.BlockSpec((None, 1, F), lambda i, be, nu: (be[i], 0, half))] + extra_specs,
                out_specs=pl.BlockSpec((tm, F), lambda i, be, nu: (i, 0)),
                scratch_shapes=[pltpu.VMEM((D, F), BF16)]),
            out_shape=jax.ShapeDtypeStruct((n_rows, F), BF16),
            compiler_params=_params(("arbitrary",)),
            name=name,
        )(blk_expert, n_used, xs, w_gu, b_gu3, *extra_in)

    glu = call(_expert_gate_kernel, "moe_gate", 0, [], [])
    act = call(_expert_up_kernel, "moe_up", 1, [glu], [pl.BlockSpec((tm, F), row_idx)])

    return pl.pallas_call(
        _expert_down_kernel,
        grid_spec=pltpu.PrefetchScalarGridSpec(
            num_scalar_prefetch=2,
            grid=(n_blk,),
            in_specs=[pl.BlockSpec((tm, F), row_idx),
                      pl.BlockSpec((None, None, F, D), lambda i, be, nu: (layer, be[i], 0, 0)),
                      pl.BlockSpec((None, 1, D), lambda i, be, nu: (be[i], 0, 0))],
            out_specs=pl.BlockSpec((tm, Dh), lambda i, be, nu: (i, 0)),
            scratch_shapes=[pltpu.VMEM((F, D), BF16)]),
        out_shape=jax.ShapeDtypeStruct((n_rows, Dh), U32),
        compiler_params=_params(("arbitrary",)),
        name="moe_down",
    )(blk_expert, n_used, act, w_down, b_down[layer].reshape(E, 1, D))


def _combine_kernel(dest_ref, ys_ref, x_ref, g5_ref, gt_ref, o_ref, buf, gcol, sem):
    tc, D = x_ref.shape
    dh = D // 2
    lc = dh // SUBLANES
    i = pl.program_id(0)
    n = pl.num_programs(0)
    n_tok = n * tc

    def copy(tile, slot, t, k):
        d = dest_ref[k * n_tok + tile * tc + t]
        return pltpu.make_async_copy(ys_ref.at[pl.ds(d, 1)], buf.at[slot, k, pl.ds(t, 1)], sem.at[slot])

    def start_tile(tile, slot):
        def body(t, _):
            for k in range(TOP_K):
                copy(tile, slot, t, k).start()
            return 0
        lax.fori_loop(0, tc, body, 0)

    def wait_tile(tile, slot):
        def body(t, _):
            for k in range(TOP_K):
                copy(tile, slot, t, k).wait()
            return 0
        lax.fori_loop(0, tc, body, 0)

    slot = i % 2

    @pl.when(i == 0)
    def _():
        start_tile(0, 0)

    wait_tile(i, slot)
    diag = lax.broadcasted_iota(I32, (tc, tc), 0) == lax.broadcasted_iota(I32, (tc, tc), 1)
    for k in range(TOP_K):
        col = jnp.sum(jnp.where(diag, gt_ref[k:k + 1, :], 0.0), axis=1, keepdims=True)
        gcol[k] = jnp.broadcast_to(col, (tc, LANES))

    def group(g, issue_next):
        r0 = pl.multiple_of(g * SUBLANES, SUBLANES)
        rows = pl.ds(r0, SUBLANES)
        cb = []
        for k in range(TOP_K):
            v = gcol[k, rows, :]
            cb.append(v[:, :lc] if lc <= LANES else jnp.concatenate([v] * (lc // LANES), axis=1))
        for t in range(SUBLANES):
            if issue_next:
                for k in range(TOP_K):
                    copy(i + 1, 1 - slot, r0 + t, k).start()
            c0 = t * lc
            acc_lo = acc_hi = None
            for k in range(TOP_K):
                lo, hi = _unpack_bf16_pair(buf[slot, k, rows, c0:c0 + lc])
                acc_lo = cb[k] * lo if acc_lo is None else acc_lo + cb[k] * lo
                acc_hi = cb[k] * hi if acc_hi is None else acc_hi + cb[k] * hi
            o_ref[rows, c0:c0 + lc] = x_ref[rows, c0:c0 + lc] + g5_ref[:, c0:c0 + lc] * acc_lo
            o_ref[rows, dh + c0:dh + c0 + lc] = (x_ref[rows, dh + c0:dh + c0 + lc]
                                                 + g5_ref[:, dh + c0:dh + c0 + lc] * acc_hi)
        return 0

    @pl.when(i + 1 < n)
    def _():
        lax.fori_loop(0, tc // SUBLANES, lambda g, _: group(g, True), 0)

    @pl.when(i + 1 >= n)
    def _():
        lax.fori_loop(0, tc // SUBLANES, lambda g, _: group(g, False), 0)


def _combine(dest_flat, ys, x2, gate5, gates, S, tc=128):
    T, D = x2.shape
    tc = _tile(S, tc)
    nt = S // tc
    return pl.pallas_call(
        _combine_kernel,
        grid_spec=pltpu.PrefetchScalarGridSpec(
            num_scalar_prefetch=1,
            grid=(T // tc,),
            in_specs=[pl.BlockSpec(memory_space=pl.ANY),
                      pl.BlockSpec((tc, D), lambda i, d: (i, 0)),
                      pl.BlockSpec((None, 1, D), lambda i, d: (i // nt, 0, 0)),
                      pl.BlockSpec((SUBLANES, tc), lambda i, d: (0, i))],
            out_specs=pl.BlockSpec((tc, D), lambda i, d: (i, 0)),
            scratch_shapes=[pltpu.VMEM((2, TOP_K, tc, D // 2), ys.dtype),
                            pltpu.VMEM((TOP_K, tc, LANES), F32),
                            pltpu.SemaphoreType.DMA((2,))]),
        out_shape=jax.ShapeDtypeStruct((T, D), F32),
        compiler_params=_params(("arbitrary",)),
        name="moe_combine",
    )(dest_flat, ys, x2, gate5, gates)


def _moe_layer(x2, gain, shift, scale, gate5, w_r, b_r, w_gu, b_gu, w_down, b_down, layer, S, tm_e=512):
    T, D = x2.shape
    E = w_r.shape[1]
    h, idx, gates, rank, cnt = _router(x2, gain, shift, scale, w_r, b_r, S)
    counts = cnt[:, 0].astype(I32)
    padded = (counts + tm_e - 1) // tm_e * tm_e
    pend = jnp.cumsum(padded)
    pstart = pend - padded
    onehot = idx[:TOP_K, :, None] == jnp.arange(E, dtype=I32)
    dest = (jnp.sum(jnp.where(onehot, pstart, 0), axis=-1) + rank[:TOP_K]).reshape(TOP_K * T)
    n_rows = (T * TOP_K // tm_e + E) * tm_e
    n_blk = n_rows // tm_e
    n_used = (pend[E - 1] // tm_e).astype(I32)
    bi = jnp.minimum(jnp.arange(n_blk, dtype=I32), n_used - 1)
    blk_expert = jnp.minimum(jnp.sum(pend[None, :] <= (bi * tm_e)[:, None], axis=1), E - 1).astype(I32)

    n_used = n_used.reshape(1)
    xs = _dispatch(dest, counts, pstart, padded, n_used, h, n_rows, tm_e)
    ys = _experts(blk_expert, n_used, xs, w_gu, b_gu, w_down, b_down, layer, tm_e)
    return _combine(dest, ys, x2, gate5, gates, S)


def _fox_qkv_kernel(x_ref, g_ref, sh_ref, sc_ref, w_ref, wf_ref, gq_ref, gk_ref, ot_ref, ok_ref, f_ref, h_ref,
                    *, n_q_tiles, hd, q_scale):
    j = pl.program_id(1)
    hpt, n_sub, _, tk = ot_ref.shape

    @pl.when(j == 0)
    def _():
        def emit(r0, rc, h):
            h_ref[pl.ds(r0, rc), :] = h.astype(BF16)
        _norm_mod_rows(x_ref, g_ref, sh_ref, sc_ref, emit)
        f_ref[...] = jnp.dot(h_ref[...], wf_ref[...], preferred_element_type=F32)

    def head_norm(yh, gain):
        inv = lax.rsqrt(jnp.mean(yh * yh, axis=-1, keepdims=True) + EPS)
        return (yh * inv) * gain

    def chunks(epilogue):
        for s in range(n_sub):
            y = jnp.dot(h_ref[s * tk:(s + 1) * tk, :], w_ref[...], preferred_element_type=F32)
            for hh in range(hpt):
                epilogue(hh, s, y[:, hh * hd:(hh + 1) * hd])

    aug_row = lax.broadcasted_iota(I32, (AUG_ROWS, tk), 0)
    q_aug = jnp.where(aug_row < 3, -1.0, 0.0).astype(ot_ref.dtype)
    v_aug = jnp.where(aug_row < 1, 1.0, 0.0).astype(ot_ref.dtype)

    @pl.when(j < n_q_tiles)
    def _():
        gain = gq_ref[...] * q_scale

        def epilogue(hh, s, yh):
            ot_ref[hh, s, 0:hd, :] = head_norm(yh, gain).T.astype(ot_ref.dtype)
            ot_ref[hh, s, hd:hd + AUG_ROWS, :] = q_aug
        chunks(epilogue)

    @pl.when((j >= n_q_tiles) & (j < 2 * n_q_tiles))
    def _():
        def epilogue(hh, s, yh):
            ok_ref[hh, s * tk:(s + 1) * tk, :] = head_norm(yh, gk_ref[...]).astype(ok_ref.dtype)
        chunks(epilogue)

    @pl.when(j >= 2 * n_q_tiles)
    def _():
        def epilogue(hh, s, yh):
            ot_ref[hh, s, 0:hd, :] = yh.T.astype(ot_ref.dtype)
            ot_ref[hh, s, hd:hd + AUG_ROWS, :] = v_aug
        chunks(epilogue)


def _fox_proj(x2, gain, shift, scale, w_in, w_f, q_gain, k_gain, B, S, H, hd, tm, tk, hpt=4):
    T, D = x2.shape
    tn = hpt * hd
    nt = S // tm
    n_sub = tm // tk
    n_q_tiles = H // hpt
    q_scale = (hd ** -0.5) * LOG2E

    nqt = n_q_tiles

    def t_idx(j):
        return jnp.where(j < nqt, j, jnp.where(j < 2 * nqt, nqt - 1, j - nqt))

    def k_idx(j):
        return jnp.clip(j - nqt, 0, nqt - 1)

    qvt, k, f = pl.pallas_call(
        functools.partial(_fox_qkv_kernel, n_q_tiles=nqt, hd=hd, q_scale=q_scale),
        grid=(T // tm, 3 * nqt),
        in_specs=[pl.BlockSpec((tm, D), lambda i, j: (i, 0)),
                  pl.BlockSpec((1, D), lambda i, j: (0, 0)),
                  pl.BlockSpec((None, 1, D), lambda i, j: (i // nt, 0, 0)),
                  pl.BlockSpec((None, 1, D), lambda i, j: (i // nt, 0, 0)),
                  pl.BlockSpec((D, tn), lambda i, j: (0, j)),
                  pl.BlockSpec((D, LANES), lambda i, j: (0, 0)),
                  pl.BlockSpec((1, hd), lambda i, j: (0, 0)),
                  pl.BlockSpec((1, hd), lambda i, j: (0, 0))],
        out_specs=[pl.BlockSpec((None, hpt, n_sub, hd + AUG_ROWS, tk),
                                lambda i, j: (i // nt, t_idx(j), i % nt, 0, 0)),
                   pl.BlockSpec((None, hpt, tm, hd), lambda i, j: (i // nt, k_idx(j), i % nt, 0)),
                   pl.BlockSpec((tm, LANES), lambda i, j: (i, 0))],
        out_shape=[jax.ShapeDtypeStruct((B, 2 * H, S // tk, hd + AUG_ROWS, tk), BF16),
                   jax.ShapeDtypeStruct((B, H, S, hd), BF16),
                   jax.ShapeDtypeStruct((T, LANES), F32)],
        scratch_shapes=[pltpu.VMEM((tm, D), BF16)],
        compiler_params=_params(("parallel", "arbitrary")),
        name="fox_qkv_proj",
    )(x2, gain, shift, scale, w_in, w_f, q_gain, k_gain)
    return qvt, k, f


def _cum_kernel(f_ref, bf_ref, o_ref, carry):
    tt = f_ref.shape[0]

    @pl.when(pl.program_id(1) == 0)
    def _():
        carry[...] = jnp.zeros_like(carry)

    logf = jax.nn.log_sigmoid(f_ref[...] + bf_ref[...])
    si = lax.broadcasted_iota(I32, (tt, tt), 0)
    ti = lax.broadcasted_iota(I32, (tt, tt), 1)
    incl = (ti <= si).astype(F32)
    c = jnp.dot(incl, logf, preferred_element_type=F32, precision=HIGHEST) + carry[0:1, :]
    carry[0:1, :] = c[tt - 1:tt, :]
    c2 = c * LOG2E
    hi = c2.astype(BF16).astype(F32)
    r1 = c2 - hi
    mid = r1.astype(BF16).astype(F32)
    lo = (r1 - mid).astype(BF16).astype(F32)
    lane = lax.broadcasted_iota(I32, (tt, LANES), 1)
    for h in range(o_ref.shape[0]):
        t = jnp.where(lane == 0, hi[:, h:h + 1],
                      jnp.where(lane == 1, mid[:, h:h + 1],
                                jnp.where(lane == 2, lo[:, h:h + 1], 0.0)))
        o_ref[h] = t.astype(o_ref.dtype)


def _cum_forget(f, b_f, B, S, H, tt=256):
    T = f.shape[0]
    tt = _tile(S, tt)
    nt = S // tt
    bf = jnp.zeros((1, LANES), F32).at[0, :H].set(b_f)
    return pl.pallas_call(
        _cum_kernel,
        grid=(B, nt),
        in_specs=[pl.BlockSpec((tt, LANES), lambda b, t: (b * nt + t, 0)),
                  pl.BlockSpec((1, LANES), lambda b, t: (0, 0))],
        out_specs=pl.BlockSpec((None, H, tt, LANES), lambda b, t: (b, 0, t, 0)),
        out_shape=jax.ShapeDtypeStruct((B, H, S, LANES), BF16),
        scratch_shapes=[pltpu.VMEM((SUBLANES, LANES), F32)],
        compiler_params=_params(("parallel", "arbitrary")),
        name="fox_cum",
    )(f, bf)


def _fox_attn_kernel(qt_ref, k_ref, c_ref, vt_ref, o_ref, s_a, s_b, p_a, p_b, m_ref, acc_ref):
    n_sub, hda, tk = qt_ref.shape
    hd = hda - AUG_ROWS
    tq = n_sub * tk
    assert n_sub == 2
    qi = pl.program_id(2)
    qt = jnp.concatenate([qt_ref[s] for s in range(n_sub)], axis=1)
    qp = jnp.concatenate([qt, jnp.zeros((hd - AUG_ROWS, tq), BF16)], axis=0)

    def scores(j):
        r0 = pl.multiple_of(j * tk, tk)
        kp = jnp.concatenate([k_ref[pl.ds(r0, tk), :], c_ref[pl.ds(r0, tk), :]], axis=1)
        return jnp.dot(kp, qp, preferred_element_type=F32)

    def stage(j, s_cur, s_nxt, p_prev, p_cur, mask=None):
        if s_nxt is not None:
            s_nxt[...] = scores(j + 1)
        pv = jnp.dot(vt_ref[jnp.maximum(j - 1, 0)], p_prev[...], preferred_element_type=F32)
        t = s_cur[...]
        if mask is not None:
            t = jnp.where(mask, t, -jnp.inf)
        m_old = m_ref[...]
        m_new = jnp.maximum(m_old, jnp.max(t, axis=0, keepdims=True))
        alpha = jnp.exp2(m_old - m_new)
        p_cur[...] = jnp.exp2(t - m_new).astype(BF16)
        acc_ref[...] = alpha * (acc_ref[...] + pv)
        m_ref[...] = m_new

    m_ref[...] = jnp.full(m_ref.shape, -jnp.inf, F32)
    acc_ref[...] = jnp.zeros(acc_ref.shape, F32)
    p_b[...] = jnp.zeros(p_b.shape, BF16)
    s_a[...] = scores(0)

    def pair(p):
        stage(2 * p, s_a, s_b, p_b, p_a)
        stage(2 * p + 1, s_b, s_a, p_a, p_b)

    def quad(i, _):
        pair(2 * i)
        pair(2 * i + 1)
        return 0

    lax.fori_loop(0, qi // 2, quad, 0)

    @pl.when(qi % 2 == 1)
    def _():
        pair(qi - 1)

    key = lax.broadcasted_iota(I32, (tk, tq), 0)
    qry = lax.broadcasted_iota(I32, (tk, tq), 1)
    stage(2 * qi, s_a, s_b, p_b, p_a, mask=key <= qry)
    stage(2 * qi + 1, s_b, None, p_a, p_b, mask=key + tk <= qry)
    acc = acc_ref[...] + jnp.dot(vt_ref[2 * qi + 1], p_b[...], preferred_element_type=F32)
    o_ref[...] = (acc[:hd] / acc[hd:hd + 1]).T.astype(o_ref.dtype)


def _fox_attn(qvt, k, caug, B, S, H, hd, tq):
    nq = S // tq
    tk = qvt.shape[-1]
    nk = S // tk
    n_sub = tq // tk
    hda = hd + AUG_ROWS
    return pl.pallas_call(
        _fox_attn_kernel,
        grid=(B, H, nq),
        in_specs=[pl.BlockSpec((None, None, n_sub, hda, tk), lambda b, h, q: (b, h, q, 0, 0)),
                  pl.BlockSpec((None, None, S, hd), lambda b, h, q: (b, h, 0, 0)),
                  pl.BlockSpec((None, None, S, LANES), lambda b, h, q: (b, h, 0, 0)),
                  pl.BlockSpec((None, None, nk, hda, tk), lambda b, h, q: (b, H + h, 0, 0, 0))],
        out_specs=pl.BlockSpec((tq, hd), lambda b, h, q: (b * nq + q, h)),
        out_shape=jax.ShapeDtypeStruct((B * S, H * hd), BF16),
        scratch_shapes=[pltpu.VMEM((tk, tq), F32), pltpu.VMEM((tk, tq), F32),
                        pltpu.VMEM((tk, tq), BF16), pltpu.VMEM((tk, tq), BF16),
                        pltpu.VMEM((1, tq), F32), pltpu.VMEM((hda, tq), F32)],
        compiler_params=_params(("parallel", "parallel", "arbitrary")),
        name="fox_attn",
    )(qvt, k, caug, qvt)


def kernel(x, c, ada_w, ada_b, ada_table, norm_mix_g, norm_ffn_g, lru_w_in, lru_conv_w, lru_conv_b, lru_w_a, lru_b_a, lru_w_x, lru_b_x, lru_lam, lru_w_out, fox_w_in, fox_b_f, fox_q_gain, fox_k_gain, fox_w_out, moe_w_router, moe_b_router, moe_w_gu, moe_b_gu, moe_w_down, moe_b_down):
    B, S, D = x.shape
    T = B * S
    depth = ada_table.shape[0]
    H = fox_b_f.shape[1]
    hd = D // H
    assert hd == LANES and ada_table.shape[1] == N_MOD

    m = _ada_proj(c, ada_w, ada_b)
    mod = (m[None] + ada_table.reshape(depth, 1, N_MOD * D)).reshape(depth, B, N_MOD, 1, D)
    x2 = x.reshape(T, D)

    for layer in range(depth):
        shift1, scale1, gate1, shift2, scale2, gate2 = (mod[layer, :, i] for i in range(N_MOD))
        g_mix = norm_mix_g[layer].reshape(1, D)
        g_ffn = norm_ffn_g[layer].reshape(1, D)
        j = layer // 2
        if layer % 2 == 0:
            nh, hb, _ = lru_w_a[j].shape
            assert hb % LANES == 0
            u = _norm_mm(x2, g_mix, shift1, scale1, lru_w_in[j].astype(BF16), S)
            w_ax = jnp.concatenate([lru_w_a[j], lru_w_x[j]], axis=-1).astype(BF16)
            b_ax = jnp.concatenate([lru_b_a[j].reshape(nh, 1, hb), lru_b_x[j].reshape(nh, 1, hb)], axis=-1)
            sp = jax.nn.softplus(-lru_lam[j]).reshape(1, nh * hb)
            y = _lru_core(u, lru_conv_w[j], lru_conv_b[j].reshape(1, -1), w_ax, b_ax, sp, B, S)
            w_out = lru_w_out[j].astype(BF16)
        else:
            tq = _tile(S, 512)
            w_in = fox_w_in[j]
            w_f = jnp.zeros((D, LANES), BF16).at[:, :H].set(w_in[:, 3 * D:].astype(BF16))
            qvt, k, f = _fox_proj(x2, g_mix, shift1, scale1, w_in.astype(BF16), w_f,
                                  fox_q_gain[j].reshape(1, hd), fox_k_gain[j].reshape(1, hd),
                                  B, S, H, hd, tq, tq // 2)
            caug = _cum_forget(f, fox_b_f[j], B, S, H)
            y = _fox_attn(qvt, k, caug, B, S, H, hd, tq)
            w_out = fox_w_out[j].astype(BF16)
        x2 = _out_proj(y, w_out, x2, gate1, S)
        x2 = _moe_layer(x2, g_ffn, shift2, scale2, gate2, moe_w_router[layer], moe_b_router[layer],
                        moe_w_gu, moe_b_gu, moe_w_down, moe_b_down, layer, S)
    return x2.reshape(B, S, D)
```

```python
import functools
import math

import jax
import jax.numpy as jnp
from jax import lax
from jax.experimental import pallas as pl
from jax.experimental.pallas import tpu as pltpu

F32 = jnp.float32
BF16 = jnp.bfloat16
I32 = jnp.int32
U32 = jnp.uint32

TOP_K = 4
N_MOD = 6
LRU_C = 8.0
SWIGLU_LIMIT = 7.0
SWIGLU_ALPHA = 1.702
EPS = 1e-6
LOG2E = 1.4426950408889634

AUG_ROWS = 16
LANES = 128
SUBLANES = 8
VMEM_LIMIT = 56 * 1024 * 1024
HIGHEST = lax.Precision.HIGHEST


def _params(sem, vmem=VMEM_LIMIT):
    return pltpu.CompilerParams(dimension_semantics=sem, vmem_limit_bytes=vmem)


def _tile(n, pref):
    t = min(n, pref)
    assert n % t == 0, (n, pref)
    return t


def _ada_kernel(ct_ref, w_ref, b_ref, o_ref, *, kc):
    K, nb = ct_ref.shape
    tn = w_ref.shape[1]

    def body(k, accs):
        r0 = pl.multiple_of(k * kc, kc)
        w = w_ref[pl.ds(r0, kc), :]
        c = ct_ref[pl.ds(r0, kc), :]
        s = c * jax.nn.sigmoid(c)
        return tuple(accs[b] + jnp.sum(w * s[:, b:b + 1], axis=0, keepdims=True) for b in range(nb))

    accs = lax.fori_loop(0, K // kc, body, tuple(jnp.zeros((1, tn), F32) for _ in range(nb)))
    for b in range(nb):
        o_ref[b:b + 1, :] = accs[b] + b_ref[...]


def _ada_proj(c, ada_w, ada_b):
    B, D = c.shape
    N = ada_w.shape[1]
    tn = _tile(N, 1024)
    kc = _tile(D, 256)
    return pl.pallas_call(
        functools.partial(_ada_kernel, kc=kc),
        grid=(N // tn,),
        in_specs=[pl.BlockSpec((D, B), lambda j: (0, 0)),
                  pl.BlockSpec((D, tn), lambda j: (0, j)),
                  pl.BlockSpec((1, tn), lambda j: (0, j))],
        out_specs=pl.BlockSpec((B, tn), lambda j: (0, j)),
        out_shape=jax.ShapeDtypeStruct((B, N), F32),
        compiler_params=_params(("parallel",)),
        name="ada_proj",
    )(c.T, ada_w, ada_b.reshape(1, N))


def _norm_mod_rows(x_ref, g_ref, sh_ref, sc_ref, emit, rc=64):
    tm = x_ref.shape[0]
    rc = min(rc, tm)
    gs = g_ref[...] * (1.0 + sc_ref[...])
    sh = sh_ref[...]

    def body(r, _):
        r0 = pl.multiple_of(r * rc, rc)
        x = x_ref[pl.ds(r0, rc), :]
        inv = lax.rsqrt(jnp.mean(x * x, axis=-1, keepdims=True) + EPS)
        emit(r0, rc, (x * inv) * gs + sh)
        return 0

    lax.fori_loop(0, tm // rc, body, 0)


def _norm_mm_kernel(x_ref, g_ref, sh_ref, sc_ref, w_ref, o_ref, h_ref):
    @pl.when(pl.program_id(1) == 0)
    def _():
        def emit(r0, rc, h):
            h_ref[pl.ds(r0, rc), :] = h.astype(BF16)
        _norm_mod_rows(x_ref, g_ref, sh_ref, sc_ref, emit)

    o_ref[...] = jnp.dot(h_ref[...], w_ref[...], preferred_element_type=F32).astype(o_ref.dtype)


def _norm_mm(x2, gain, shift, scale, w, S, tm=512, tn=1024):
    T, D = x2.shape
    N = w.shape[1]
    tm = _tile(S, tm)
    tn = _tile(N, tn)
    nt = S // tm
    return pl.pallas_call(
        _norm_mm_kernel,
        grid=(T // tm, N // tn),
        in_specs=[pl.BlockSpec((tm, D), lambda i, j: (i, 0)),
                  pl.BlockSpec((1, D), lambda i, j: (0, 0)),
                  pl.BlockSpec((None, 1, D), lambda i, j: (i // nt, 0, 0)),
                  pl.BlockSpec((None, 1, D), lambda i, j: (i // nt, 0, 0)),
                  pl.BlockSpec((D, tn), lambda i, j: (0, j))],
        out_specs=pl.BlockSpec((tm, tn), lambda i, j: (i, j)),
        out_shape=jax.ShapeDtypeStruct((T, N), BF16),
        scratch_shapes=[pltpu.VMEM((tm, D), BF16)],
        compiler_params=_params(("parallel", "arbitrary")),
        name="norm_mm",
    )(x2, gain, shift, scale, w)


def _gelu_tanh(x):
    c2 = 2.0 * math.sqrt(2.0 / math.pi)
    z2 = x * (c2 + (c2 * 0.044715) * (x * x))
    return x / (1.0 + jnp.exp(-z2))


def _lru_kernel(xb_ref, gb_ref, cw_ref, cb_ref, wax_ref, bax_ref, sp_ref, o_ref,
                xpad, a_s, b_s, hl_s, ac_s, carry):
    tt, C = xb_ref.shape
    L = tt // SUBLANES

    @pl.when(pl.program_id(2) == 0)
    def _():
        xpad[0:SUBLANES, :] = jnp.zeros((SUBLANES, C), F32)
        carry[...] = jnp.zeros_like(carry)

    xpad[SUBLANES:SUBLANES + tt, :] = xb_ref[...].astype(F32)
    cw = cw_ref[...]
    nk = cw.shape[0]
    xc = cb_ref[...] + cw[nk - 1:nk, :] * xpad[SUBLANES:SUBLANES + tt, :]
    for d in range(1, nk):
        xc = xc + cw[nk - 1 - d:nk - d, :] * xpad[SUBLANES - d:SUBLANES - d + tt, :]
    xpad[0:SUBLANES, :] = xpad[tt:tt + SUBLANES, :]

    ri = jnp.dot(xc.astype(BF16), wax_ref[...], preferred_element_type=F32) + bax_ref[...]
    r = jax.nn.sigmoid(ri[:, :C])
    ig = jax.nn.sigmoid(ri[:, C:])
    log_a = (-LRU_C) * r * sp_ref[...]
    a = jnp.exp(log_a)
    om = 1.0 - a * a
    mult = jnp.where(om > 0.0, om * lax.rsqrt(om), 0.0)
    nlv = C // LANES
    bt = mult * (ig * xc)
    for c in range(nlv):
        a_s[c] = a[:, c * LANES:(c + 1) * LANES]
        b_s[c] = bt[:, c * LANES:(c + 1) * LANES]

    def body(j, hA):
        j8 = pl.multiple_of(j * SUBLANES, SUBLANES)
        out = []
        for c in range(nlv):
            h, A = hA[c]
            aj = a_s[c, pl.ds(j, SUBLANES, stride=L), :]
            bj = b_s[c, pl.ds(j, SUBLANES, stride=L), :]
            h = aj * h + bj
            A = aj * A
            hl_s[c, pl.ds(j8, SUBLANES), :] = h
            ac_s[c, pl.ds(j8, SUBLANES), :] = A
            out.append((h, A))
        return tuple(out)

    init = tuple((jnp.zeros((SUBLANES, LANES), F32), jnp.ones((SUBLANES, LANES), F32)) for _ in range(nlv))
    ends = lax.fori_loop(0, L, body, init, unroll=min(L, 8))

    for c in range(nlv):
        h_end, a_end = ends[c]
        cs = slice(c * LANES, (c + 1) * LANES)
        e = carry[0:1, cs]
        for s in range(SUBLANES):
            hl = hl_s[c, pl.ds(s, L, stride=SUBLANES), :]
            ac = ac_s[c, pl.ds(s, L, stride=SUBLANES), :]
            hs = hl + ac * e
            gate = _gelu_tanh(gb_ref[s * L:(s + 1) * L, cs].astype(F32))
            o_ref[s * L:(s + 1) * L, cs] = (hs * gate).astype(o_ref.dtype)
            e = h_end[s:s + 1, :] + a_end[s:s + 1, :] * e
        carry[0:1, cs] = e


def _lru_core(u, conv_w, conv_b, w_ax, b_ax, softplus_neg_lam, B, S, tt=2048):
    T, C2 = u.shape
    C = C2 // 2
    nh, hb, _ = w_ax.shape
    tt = _tile(S, tt)
    nt = S // tt
    return pl.pallas_call(
        _lru_kernel,
        grid=(B, nh, nt),
        in_specs=[pl.BlockSpec((tt, hb), lambda b, h, t: (b * nt + t, h)),
                  pl.BlockSpec((tt, hb), lambda b, h, t: (b * nt + t, nh + h)),
                  pl.BlockSpec((conv_w.shape[0], hb), lambda b, h, t: (0, h)),
                  pl.BlockSpec((1, hb), lambda b, h, t: (0, h)),
                  pl.BlockSpec((None, hb, 2 * hb), lambda b, h, t: (h, 0, 0)),
                  pl.BlockSpec((None, 1, 2 * hb), lambda b, h, t: (h, 0, 0)),
                  pl.BlockSpec((1, hb), lambda b, h, t: (0, h))],
        out_specs=pl.BlockSpec((tt, hb), lambda b, h, t: (b * nt + t, h)),
        out_shape=jax.ShapeDtypeStruct((T, C), BF16),
        scratch_shapes=[pltpu.VMEM((tt + SUBLANES, hb), F32),
                        pltpu.VMEM((hb // LANES, tt, LANES), F32), pltpu.VMEM((hb // LANES, tt, LANES), F32),
                        pltpu.VMEM((hb // LANES, tt, LANES), F32), pltpu.VMEM((hb // LANES, tt, LANES), F32),
                        pltpu.VMEM((SUBLANES, hb), F32)],
        compiler_params=_params(("parallel", "parallel", "arbitrary")),
        name="lru_core",
    )(u, u, conv_w, conv_b, w_ax, b_ax, softplus_neg_lam)


def _out_proj_kernel(y_ref, w_ref, x_ref, g_ref, o_ref):
    o_ref[...] = x_ref[...] + g_ref[...] * jnp.dot(y_ref[...], w_ref[...], preferred_element_type=F32)


def _out_proj(y, w, x2, gate, S, tm=1024, tn=1024):
    T, D = x2.shape
    K = y.shape[1]
    tm = _tile(S, tm)
    tn = _tile(D, tn)
    nt = S // tm
    return pl.pallas_call(
        _out_proj_kernel,
        grid=(T // tm, D // tn),
        in_specs=[pl.BlockSpec((tm, K), lambda i, j: (i, 0)),
                  pl.BlockSpec((K, tn), lambda i, j: (0, j)),
                  pl.BlockSpec((tm, tn), lambda i, j: (i, j)),
                  pl.BlockSpec((None, 1, tn), lambda i, j: (i // nt, 0, j))],
        out_specs=pl.BlockSpec((tm, tn), lambda i, j: (i, j)),
        out_shape=jax.ShapeDtypeStruct((T, D), F32),
        compiler_params=_params(("parallel", "parallel")),
        name="out_proj",
    )(y, w, x2, gate)


def _pack_bf16_pair(lo, hi):
    ul = lax.bitcast_convert_type(lo, U32)
    uh = lax.bitcast_convert_type(hi, U32)
    rl = (ul + 0x8000) >> 16
    rh = ((uh + 0x8000) >> 16) << 16
    return rh | rl


def _unpack_bf16_pair(w):
    lo = lax.bitcast_convert_type(w << 16, F32)
    hi = lax.bitcast_convert_type((w >> 16) << 16, F32)
    return lo, hi


def _unpack_rows_bf16(w):
    lo, hi = _unpack_bf16_pair(w)
    return jnp.concatenate([lo.astype(BF16), hi.astype(BF16)], axis=1)


def _router_kernel(x_ref, g_ref, sh_ref, sc_ref, wr_ref, br_ref,
                   hp_ref, idx_ref, gate_ref, rank_ref, cnt_ref, carry, lg_ref, *, n_exp):
    tm, D = x_ref.shape

    @pl.when(pl.program_id(0) == 0)
    def _():
        carry[...] = jnp.zeros_like(carry)

    def emit(r0, rc, h):
        hp_ref[pl.ds(r0, rc), :] = _pack_bf16_pair(h[:, :D // 2], h[:, D // 2:])
        lg_ref[pl.ds(r0, rc), :] = jnp.dot(h.astype(BF16), wr_ref[...], preferred_element_type=F32)
    _norm_mod_rows(x_ref, g_ref, sh_ref, sc_ref, emit)

    logits = lg_ref[...] + br_ref[...]
    v = logits.T[:n_exp, :]
    eio = lax.broadcasted_iota(I32, (n_exp, tm), 0)
    vals, idxs = [], []
    for _ in range(TOP_K):
        m = jnp.max(v, axis=0, keepdims=True)
        ik = jnp.min(jnp.where(v == m, eio, n_exp), axis=0, keepdims=True)
        vals.append(m)
        idxs.append(ik)
        v = jnp.where(eio == ik, -jnp.inf, v)
    ex = [jnp.exp(vk - vals[0]) for vk in vals]
    den = ex[0] + ex[1] + ex[2] + ex[3]

    onehot = jnp.zeros((n_exp, tm), F32)
    for ik in idxs:
        onehot = onehot + (eio == ik).astype(F32)
    si = lax.broadcasted_iota(I32, (tm, tm), 0)
    ti = lax.broadcasted_iota(I32, (tm, tm), 1)
    before = (si < ti).astype(BF16)
    base = jnp.dot(onehot.astype(BF16), before, preferred_element_type=F32) + carry[:, 0:1]

    zi = jnp.zeros((SUBLANES - TOP_K, tm), I32)
    idx_ref[...] = jnp.concatenate(idxs + [zi], axis=0)
    gate_ref[...] = jnp.concatenate([e / den for e in ex] + [zi.astype(F32)], axis=0)
    ranks = [jnp.sum(jnp.where(eio == ik, base, 0.0), axis=0, keepdims=True).astype(I32) for ik in idxs]
    rank_ref[...] = jnp.concatenate(ranks + [zi], axis=0)

    carry[...] = carry[...] + jnp.sum(onehot, axis=1, keepdims=True)
    cnt_ref[...] = carry[...]


def _router(x2, gain, shift, scale, w_r, b_r, S, tm=512):
    T, D = x2.shape
    E = w_r.shape[1]
    tm = _tile(S, tm)
    nt = S // tm
    wr = jnp.zeros((D, LANES), BF16).at[:, :E].set(w_r.astype(BF16))
    br = jnp.zeros((1, LANES), F32).at[0, :E].set(b_r)
    return pl.pallas_call(
        functools.partial(_router_kernel, n_exp=E),
        grid=(T // tm,),
        in_specs=[pl.BlockSpec((tm, D), lambda i: (i, 0)),
                  pl.BlockSpec((1, D), lambda i: (0, 0)),
                  pl.BlockSpec((None, 1, D), lambda i: (i // nt, 0, 0)),
                  pl.BlockSpec((None, 1, D), lambda i: (i // nt, 0, 0)),
                  pl.BlockSpec((D, LANES), lambda i: (0, 0)),
                  pl.BlockSpec((1, LANES), lambda i: (0, 0))],
        out_specs=[pl.BlockSpec((tm, D // 2), lambda i: (i, 0)),
                   pl.BlockSpec((SUBLANES, tm), lambda i: (0, i)),
                   pl.BlockSpec((SUBLANES, tm), lambda i: (0, i)),
                   pl.BlockSpec((SUBLANES, tm), lambda i: (0, i)),
                   pl.BlockSpec((E, LANES), lambda i: (0, 0))],
        out_shape=[jax.ShapeDtypeStruct((T, D // 2), U32),
                   jax.ShapeDtypeStruct((SUBLANES, T), I32),
                   jax.ShapeDtypeStruct((SUBLANES, T), F32),
                   jax.ShapeDtypeStruct((SUBLANES, T), I32),
                   jax.ShapeDtypeStruct((E, LANES), F32)],
        scratch_shapes=[pltpu.VMEM((E, LANES), F32), pltpu.VMEM((tm, LANES), F32)],
        compiler_params=_params(("arbitrary",)),
        name="moe_router",
    )(x2, gain, shift, scale, wr, br)


def _dispatch_kernel(dest_ref, cnt_ref, pst_ref, pad_ref, nu_ref, h_ref, xs_ref, zblk, sem):
    tc = h_ref.shape[0]
    tm_e = zblk.shape[0]
    i = pl.program_id(0)
    n_tok = pl.num_programs(0) * tc
    n_exp = cnt_ref.shape[0]
    n_blk = xs_ref.shape[0] // tm_e

    def copy(t, k):
        d = dest_ref[k * n_tok + i * tc + t]
        return pltpu.make_async_copy(h_ref.at[pl.ds(t, 1)], xs_ref.at[pl.ds(d, 1)], sem.at[0])

    def pad_copy(r):
        return pltpu.make_async_copy(zblk.at[pl.ds(0, 1)], xs_ref.at[pl.ds(r, 1)], sem.at[1])

    def blk_copy(b):
        r0 = pl.multiple_of(b * tm_e, tm_e)
        return pltpu.make_async_copy(zblk, xs_ref.at[pl.ds(r0, tm_e)], sem.at[2])

    def for_pad_rows(fn):
        def per_expert(e, _):
            def per_row(r, _):
                fn(pad_copy(r))
                return 0
            lax.fori_loop(pst_ref[e] + cnt_ref[e], pst_ref[e] + pad_ref[e], per_row, 0)
            return 0
        lax.fori_loop(0, n_exp, per_expert, 0)

        def per_blk(b, _):
            fn(blk_copy(b))
            return 0
        lax.fori_loop(nu_ref[0], n_blk, per_blk, 0)

    @pl.when(i == 0)
    def _():
        zblk[...] = jnp.zeros(zblk.shape, zblk.dtype)
        for_pad_rows(lambda c: c.start())

    def start(t, _):
        for k in range(TOP_K):
            copy(t, k).start()
        return 0

    def wait(t, _):
        for k in range(TOP_K):
            copy(t, k).wait()
        return 0

    lax.fori_loop(0, tc, start, 0)
    lax.fori_loop(0, tc, wait, 0)

    @pl.when(i == 0)
    def _():
        for_pad_rows(lambda c: c.wait())


def _dispatch(dest_flat, counts, pstart, padded, n_used, h, n_rows, tm_e, tc=256):
    T, D = h.shape
    tc = _tile(T, tc)
    return pl.pallas_call(
        _dispatch_kernel,
        grid_spec=pltpu.PrefetchScalarGridSpec(
            num_scalar_prefetch=5,
            grid=(T // tc,),
            in_specs=[pl.BlockSpec((tc, D), lambda i, *_: (i, 0))],
            out_specs=pl.BlockSpec(memory_space=pl.ANY),
            scratch_shapes=[pltpu.VMEM((tm_e, D), h.dtype), pltpu.SemaphoreType.DMA((3,))]),
        out_shape=jax.ShapeDtypeStruct((n_rows, D), h.dtype),
        compiler_params=_params(("arbitrary",)),
        name="moe_dispatch",
    )(dest_flat, counts, pstart, padded, n_used, h)


def _refresh_expert_weight(be_ref, w_ref, wbf_ref, rows=256):
    i = pl.program_id(0)
    changed = (i == 0) | (be_ref[i] != be_ref[jnp.maximum(i - 1, 0)])

    @pl.when(changed)
    def _():
        rc = min(rows, w_ref.shape[0])

        def body(r, _):
            r0 = pl.multiple_of(r * rc, rc)
            wbf_ref[pl.ds(r0, rc), :] = w_ref[pl.ds(r0, rc), :].astype(BF16)
            return 0

        lax.fori_loop(0, w_ref.shape[0] // rc, body, 0)


def _expert_gate_kernel(be_ref, nu_ref, xs_ref, w_ref, b_ref, o_ref, wbf_ref):
    _refresh_expert_weight(be_ref, w_ref, wbf_ref)
    i = pl.program_id(0)

    @pl.when(i < nu_ref[0])
    def _():
        g = jnp.dot(_unpack_rows_bf16(xs_ref[...]), wbf_ref[...], preferred_element_type=F32) + b_ref[...]
        g = jnp.minimum(g, SWIGLU_LIMIT)
        o_ref[...] = (g * jax.nn.sigmoid(SWIGLU_ALPHA * g)).astype(o_ref.dtype)

    @pl.when(i >= nu_ref[0])
    def _():
        o_ref[...] = jnp.zeros_like(o_ref)


def _expert_up_kernel(be_ref, nu_ref, xs_ref, w_ref, b_ref, glu_ref, o_ref, wbf_ref):
    _refresh_expert_weight(be_ref, w_ref, wbf_ref)
    i = pl.program_id(0)

    @pl.when(i < nu_ref[0])
    def _():
        up = jnp.dot(_unpack_rows_bf16(xs_ref[...]), wbf_ref[...], preferred_element_type=F32) + b_ref[...]
        up = jnp.clip(up, -SWIGLU_LIMIT, SWIGLU_LIMIT)
        o_ref[...] = ((up + 1.0) * glu_ref[...].astype(F32)).astype(o_ref.dtype)

    @pl.when(i >= nu_ref[0])
    def _():
        o_ref[...] = jnp.zeros_like(o_ref)


def _expert_down_kernel(be_ref, nu_ref, a_ref, w_ref, b_ref, o_ref, wbf_ref):
    _refresh_expert_weight(be_ref, w_ref, wbf_ref)
    i = pl.program_id(0)

    @pl.when(i < nu_ref[0])
    def _():
        y = jnp.dot(a_ref[...], wbf_ref[...], preferred_element_type=F32) + b_ref[...]
        dh = y.shape[1] // 2
        o_ref[...] = _pack_bf16_pair(y[:, :dh], y[:, dh:])

    @pl.when(i >= nu_ref[0])
    def _():
        o_ref[...] = jnp.zeros_like(o_ref)


def _experts(blk_expert, n_used, xs, w_gu, b_gu, w_down, b_down, layer, tm):
    n_rows, Dh = xs.shape
    _, E, D, F2 = w_gu.shape
    F = F2 // 2
    n_blk = n_rows // tm
    b_gu3 = b_gu[layer].reshape(E, 1, F2)

    def row_idx(i, be, nu):
        return (jnp.minimum(i, nu[0] - 1), 0)

    def call(kern, name, half, extra_in, extra_specs):
        return pl.pallas_call(
            kern,
            grid_spec=pltpu.PrefetchScalarGridSpec(
                num_scalar_prefetch=2,
                grid=(n_blk,),
                in_specs=[pl.BlockSpec((tm, Dh), row_idx),
                          pl.BlockSpec((None, None, D, F), lambda i, be, nu: (layer, be[i], 0, half)),
                          pl.BlockSpec((None, 1, F), lambda i, be, nu: (be[i], 0, half))] + extra_specs,
                out_specs=pl.BlockSpec((tm, F), lambda i, be, nu: (i, 0)),
                scratch_shapes=[pltpu.VMEM((D, F), BF16)]),
            out_shape=jax.ShapeDtypeStruct((n_rows, F), BF16),
            compiler_params=_params(("arbitrary",)),
            name=name,
        )(blk_expert, n_used, xs, w_gu, b_gu3, *extra_in)

    glu = call(_expert_gate_kernel, "moe_gate", 0, [], [])
    act = call(_expert_up_kernel, "moe_up", 1, [glu], [pl.BlockSpec((tm, F), row_idx)])

    return pl.pallas_call(
        _expert_down_kernel,
        grid_spec=pltpu.PrefetchScalarGridSpec(
            num_scalar_prefetch=2,
            grid=(n_blk,),
            in_specs=[pl.BlockSpec((tm, F), row_idx),
                      pl.BlockSpec((None, None, F, D), lambda i, be, nu: (layer, be[i], 0, 0)),
                      pl.BlockSpec((None, 1, D), lambda i, be, nu: (be[i], 0, 0))],
            out_specs=pl.BlockSpec((tm, Dh), lambda i, be, nu: (i, 0)),
            scratch_shapes=[pltpu.VMEM((F, D), BF16)]),
        out_shape=jax.ShapeDtypeStruct((n_rows, Dh), U32),
        compiler_params=_params(("arbitrary",)),
        name="moe_down",
    )(blk_expert, n_used, act, w_down, b_down[layer].reshape(E, 1, D))


def _combine_kernel(dest_ref, ys_ref, x_ref, g5_ref, gt_ref, o_ref, buf, gcol, sem):
    tc, D = x_ref.shape
    dh = D // 2
    lc = dh // SUBLANES
    i = pl.program_id(0)
    n = pl.num_programs(0)
    n_tok = n * tc

    def copy(tile, slot, t, k):
        d = dest_ref[k * n_tok + tile * tc + t]
        return pltpu.make_async_copy(ys_ref.at[pl.ds(d, 1)], buf.at[slot, k, pl.ds(t, 1)], sem.at[slot])

    def start_tile(tile, slot):
        def body(t, _):
            for k in range(TOP_K):
                copy(tile, slot, t, k).start()
            return 0
        lax.fori_loop(0, tc, body, 0)

    def wait_tile(tile, slot):
        def body(t, _):
            for k in range(TOP_K):
                copy(tile, slot, t, k).wait()
            return 0
        lax.fori_loop(0, tc, body, 0)

    slot = i % 2

    @pl.when(i == 0)
    def _():
        start_tile(0, 0)

    wait_tile(i, slot)
    diag = lax.broadcasted_iota(I32, (tc, tc), 0) == lax.broadcasted_iota(I32, (tc, tc), 1)
    for k in range(TOP_K):
        col = jnp.sum(jnp.where(diag, gt_ref[k:k + 1, :], 0.0), axis=1, keepdims=True)
        gcol[k] = jnp.broadcast_to(col, (tc, LANES))

    def group(g, issue_next):
        r0 = pl.multiple_of(g * SUBLANES, SUBLANES)
        rows = pl.ds(r0, SUBLANES)
        cb = []
        for k in range(TOP_K):
            v = gcol[k, rows, :]
            cb.append(v[:, :lc] if lc <= LANES else jnp.concatenate([v] * (lc // LANES), axis=1))
        for t in range(SUBLANES):
            if issue_next:
                for k in range(TOP_K):
                    copy(i + 1, 1 - slot, r0 + t, k).start()
            c0 = t * lc
            acc_lo = acc_hi = None
            for k in range(TOP_K):
                lo, hi = _unpack_bf16_pair(buf[slot, k, rows, c0:c0 + lc])
                acc_lo = cb[k] * lo if acc_lo is None else acc_lo + cb[k] * lo
                acc_hi = cb[k] * hi if acc_hi is None else acc_hi + cb[k] * hi
            o_ref[rows, c0:c0 + lc] = x_ref[rows, c0:c0 + lc] + g5_ref[:, c0:c0 + lc] * acc_lo
            o_ref[rows, dh + c0:dh + c0 + lc] = (x_ref[rows, dh + c0:dh + c0 + lc]
                                                 + g5_ref[:, dh + c0:dh + c0 + lc] * acc_hi)
        return 0

    @pl.when(i + 1 < n)
    def _():
        lax.fori_loop(0, tc // SUBLANES, lambda g, _: group(g, True), 0)

    @pl.when(i + 1 >= n)
    def _():
        lax.fori_loop(0, tc // SUBLANES, lambda g, _: group(g, False), 0)


def _combine(dest_flat, ys, x2, gate5, gates, S, tc=128):
    T, D = x2.shape
    tc = _tile(S, tc)
    nt = S // tc
    return pl.pallas_call(
        _combine_kernel,
        grid_spec=pltpu.PrefetchScalarGridSpec(
            num_scalar_prefetch=1,
            grid=(T // tc,),
            in_specs=[pl.BlockSpec(memory_space=pl.ANY),
                      pl.BlockSpec((tc, D), lambda i, d: (i, 0)),
                      pl.BlockSpec((None, 1, D), lambda i, d: (i // nt, 0, 0)),
                      pl.BlockSpec((SUBLANES, tc), lambda i, d: (0, i))],
            out_specs=pl.BlockSpec((tc, D), lambda i, d: (i, 0)),
            scratch_shapes=[pltpu.VMEM((2, TOP_K, tc, D // 2), ys.dtype),
                            pltpu.VMEM((TOP_K, tc, LANES), F32),
                            pltpu.SemaphoreType.DMA((2,))]),
        out_shape=jax.ShapeDtypeStruct((T, D), F32),
        compiler_params=_params(("arbitrary",)),
        name="moe_combine",
    )(dest_flat, ys, x2, gate5, gates)


def _moe_layer(x2, gain, shift, scale, gate5, w_r, b_r, w_gu, b_gu, w_down, b_down, layer, S, tm_e=512):
    T, D = x2.shape
    E = w_r.shape[1]
    h, idx, gates, rank, cnt = _router(x2, gain, shift, scale, w_r, b_r, S)
    counts = cnt[:, 0].astype(I32)
    padded = (counts + tm_e - 1) // tm_e * tm_e
    pend = jnp.cumsum(padded)
    pstart = pend - padded
    onehot = idx[:TOP_K, :, None] == jnp.arange(E, dtype=I32)
    dest = (jnp.sum(jnp.where(onehot, pstart, 0), axis=-1) + rank[:TOP_K]).reshape(TOP_K * T)
    n_rows = (T * TOP_K // tm_e + E) * tm_e
    n_blk = n_rows // tm_e
    n_used = (pend[E - 1] // tm_e).astype(I32)
    bi = jnp.minimum(jnp.arange(n_blk, dtype=I32), n_used - 1)
    blk_expert = jnp.minimum(jnp.sum(pend[None, :] <= (bi * tm_e)[:, None], axis=1), E - 1).astype(I32)

    n_used = n_used.reshape(1)
    xs = _dispatch(dest, counts, pstart, padded, n_used, h, n_rows, tm_e)
    ys = _experts(blk_expert, n_used, xs, w_gu, b_gu, w_down, b_down, layer, tm_e)
    return _combine(dest, ys, x2, gate5, gates, S)


def _fox_qkv_kernel(x_ref, g_ref, sh_ref, sc_ref, w_ref, wf_ref, gq_ref, gk_ref, ot_ref, ok_ref, f_ref, h_ref,
                    *, n_q_tiles, hd, q_scale):
    j = pl.program_id(1)
    hpt, n_sub, _, tk = ot_ref.shape

    @pl.when(j == 0)
    def _():
        def emit(r0, rc, h):
            h_ref[pl.ds(r0, rc), :] = h.astype(BF16)
        _norm_mod_rows(x_ref, g_ref, sh_ref, sc_ref, emit)
        f_ref[...] = jnp.dot(h_ref[...], wf_ref[...], preferred_element_type=F32)

    def head_norm(yh, gain):
        inv = lax.rsqrt(jnp.mean(yh * yh, axis=-1, keepdims=True) + EPS)
        return (yh * inv) * gain

    def chunks(epilogue):
        for s in range(n_sub):
            y = jnp.dot(h_ref[s * tk:(s + 1) * tk, :], w_ref[...], preferred_element_type=F32)
            for hh in range(hpt):
                epilogue(hh, s, y[:, hh * hd:(hh + 1) * hd])

    aug_row = lax.broadcasted_iota(I32, (AUG_ROWS, tk), 0)
    q_aug = jnp.where(aug_row < 3, -1.0, 0.0).astype(ot_ref.dtype)
    v_aug = jnp.where(aug_row < 1, 1.0, 0.0).astype(ot_ref.dtype)

    @pl.when(j < n_q_tiles)
    def _():
        gain = gq_ref[...] * q_scale

        def epilogue(hh, s, yh):
            ot_ref[hh, s, 0:hd, :] = head_norm(yh, gain).T.astype(ot_ref.dtype)
            ot_ref[hh, s, hd:hd + AUG_ROWS, :] = q_aug
        chunks(epilogue)

    @pl.when((j >= n_q_tiles) & (j < 2 * n_q_tiles))
    def _():
        def epilogue(hh, s, yh):
            ok_ref[hh, s * tk:(s + 1) * tk, :] = head_norm(yh, gk_ref[...]).astype(ok_ref.dtype)
        chunks(epilogue)

    @pl.when(j >= 2 * n_q_tiles)
    def _():
        def epilogue(hh, s, yh):
            ot_ref[hh, s, 0:hd, :] = yh.T.astype(ot_ref.dtype)
            ot_ref[hh, s, hd:hd + AUG_ROWS, :] = v_aug
        chunks(epilogue)


def _fox_proj(x2, gain, shift, scale, w_in, w_f, q_gain, k_gain, B, S, H, hd, tm, tk, hpt=8):
    T, D = x2.shape
    hpt = _tile(H, hpt)
    tn = hpt * hd
    nt = S // tm
    n_sub = tm // tk
    n_q_tiles = H // hpt
    q_scale = (hd ** -0.5) * LOG2E

    nqt = n_q_tiles

    def t_idx(j):
        return jnp.where(j < nqt, j, jnp.where(j < 2 * nqt, nqt - 1, j - nqt))

    def k_idx(j):
        return jnp.clip(j - nqt, 0, nqt - 1)

    qvt, k, f = pl.pallas_call(
        functools.partial(_fox_qkv_kernel, n_q_tiles=nqt, hd=hd, q_scale=q_scale),
        grid=(T // tm, 3 * nqt),
        in_specs=[pl.BlockSpec((tm, D), lambda i, j: (i, 0)),
                  pl.BlockSpec((1, D), lambda i, j: (0, 0)),
                  pl.BlockSpec((None, 1, D), lambda i, j: (i // nt, 0, 0)),
                  pl.BlockSpec((None, 1, D), lambda i, j: (i // nt, 0, 0)),
                  pl.BlockSpec((D, tn), lambda i, j: (0, j)),
                  pl.BlockSpec((D, LANES), lambda i, j: (0, 0)),
                  pl.BlockSpec((1, hd), lambda i, j: (0, 0)),
                  pl.BlockSpec((1, hd), lambda i, j: (0, 0))],
        out_specs=[pl.BlockSpec((None, hpt, n_sub, hd + AUG_ROWS, tk),
                                lambda i, j: (i // nt, t_idx(j), i % nt, 0, 0)),
                   pl.BlockSpec((None, hpt, tm, hd), lambda i, j: (i // nt, k_idx(j), i % nt, 0)),
                   pl.BlockSpec((tm, LANES), lambda i, j: (i, 0))],
        out_shape=[jax.ShapeDtypeStruct((B, 2 * H, S // tk, hd + AUG_ROWS, tk), BF16),
                   jax.ShapeDtypeStruct((B, H, S, hd), BF16),
                   jax.ShapeDtypeStruct((T, LANES), F32)],
        scratch_shapes=[pltpu.VMEM((tm, D), BF16)],
        compiler_params=_params(("parallel", "arbitrary")),
        name="fox_qkv_proj",
    )(x2, gain, shift, scale, w_in, w_f, q_gain, k_gain)
    return qvt, k, f


def _cum_kernel(f_ref, bf_ref, o_ref, carry):
    tt = f_ref.shape[0]

    @pl.when(pl.program_id(1) == 0)
    def _():
        carry[...] = jnp.zeros_like(carry)

    logf = jax.nn.log_sigmoid(f_ref[...] + bf_ref[...])
    si = lax.broadcasted_iota(I32, (tt, tt), 0)
    ti = lax.broadcasted_iota(I32, (tt, tt), 1)
    incl = (ti <= si).astype(F32)
    c = jnp.dot(incl, logf, preferred_element_type=F32, precision=HIGHEST) + carry[0:1, :]
    carry[0:1, :] = c[tt - 1:tt, :]
    c2 = c * LOG2E
    hi = c2.astype(BF16).astype(F32)
    r1 = c2 - hi
    mid = r1.astype(BF16).astype(F32)
    lo = (r1 - mid).astype(BF16).astype(F32)
    lane = lax.broadcasted_iota(I32, (tt, LANES), 1)
    for h in range(o_ref.shape[0]):
        t = jnp.where(lane == 0, hi[:, h:h + 1],
                      jnp.where(lane == 1, mid[:, h:h + 1],
                                jnp.where(lane == 2, lo[:, h:h + 1], 0.0)))
        o_ref[h] = t.astype(o_ref.dtype)


def _cum_forget(f, b_f, B, S, H, tt=256):
    T = f.shape[0]
    tt = _tile(S, tt)
    nt = S // tt
    bf = jnp.zeros((1, LANES), F32).at[0, :H].set(b_f)
    return pl.pallas_call(
        _cum_kernel,
        grid=(B, nt),
        in_specs=[pl.BlockSpec((tt, LANES), lambda b, t: (b * nt + t, 0)),
                  pl.BlockSpec((1, LANES), lambda b, t: (0, 0))],
        out_specs=pl.BlockSpec((None, H, tt, LANES), lambda b, t: (b, 0, t, 0)),
        out_shape=jax.ShapeDtypeStruct((B, H, S, LANES), BF16),
        scratch_shapes=[pltpu.VMEM((SUBLANES, LANES), F32)],
        compiler_params=_params(("parallel", "arbitrary")),
        name="fox_cum",
    )(f, bf)


def _fox_attn_kernel(qt_ref, k_ref, c_ref, vt_ref, o_ref, *scratch):
    hd = k_ref.shape[2]
    for g in range(qt_ref.shape[0]):
        _fox_attn_head(qt_ref.at[g], k_ref.at[g], c_ref.at[g], vt_ref.at[g],
                       o_ref.at[:, g * hd:(g + 1) * hd], *scratch)


def _fox_attn_head(qt_ref, k_ref, c_ref, vt_ref, o_ref, s_a, s_b, p_a, p_b, m_ref, acc_ref):
    n_sub, hda, tk = qt_ref.shape
    hd = hda - AUG_ROWS
    tq = n_sub * tk
    assert n_sub == 2
    qi = pl.program_id(2)
    qt = jnp.concatenate([qt_ref[s] for s in range(n_sub)], axis=1)
    qp = jnp.concatenate([qt, jnp.zeros((hd - AUG_ROWS, tq), BF16)], axis=0)

    def scores(j):
        r0 = pl.multiple_of(j * tk, tk)
        kp = jnp.concatenate([k_ref[pl.ds(r0, tk), :], c_ref[pl.ds(r0, tk), :]], axis=1)
        return jnp.dot(kp, qp, preferred_element_type=F32)

    def stage(j, s_cur, s_nxt, p_prev, p_cur, mask=None):
        if s_nxt is not None:
            s_nxt[...] = scores(j + 1)
        pv = jnp.dot(vt_ref[jnp.maximum(j - 1, 0)], p_prev[...], preferred_element_type=F32)
        t = s_cur[...]
        if mask is not None:
            t = jnp.where(mask, t, -jnp.inf)
        m_old = m_ref[...]
        m_new = jnp.maximum(m_old, jnp.max(t, axis=0, keepdims=True))
        alpha = jnp.exp2(m_old - m_new)
        p_cur[...] = jnp.exp2(t - m_new).astype(BF16)
        acc_ref[...] = alpha * (acc_ref[...] + pv)
        m_ref[...] = m_new

    m_ref[...] = jnp.full(m_ref.shape, -jnp.inf, F32)
    acc_ref[...] = jnp.zeros(acc_ref.shape, F32)
    p_b[...] = jnp.zeros(p_b.shape, BF16)
    s_a[...] = scores(0)

    def pair(p):
        stage(2 * p, s_a, s_b, p_b, p_a)
        stage(2 * p + 1, s_b, s_a, p_a, p_b)

    def quad(i, _):
        pair(2 * i)
        pair(2 * i + 1)
        return 0

    lax.fori_loop(0, qi // 2, quad, 0)

    @pl.when(qi % 2 == 1)
    def _():
        pair(qi - 1)

    key = lax.broadcasted_iota(I32, (tk, tq), 0)
    qry = lax.broadcasted_iota(I32, (tk, tq), 1)
    stage(2 * qi, s_a, s_b, p_b, p_a, mask=key <= qry)
    stage(2 * qi + 1, s_b, None, p_a, p_b, mask=key + tk <= qry)
    acc = acc_ref[...] + jnp.dot(vt_ref[2 * qi + 1], p_b[...], preferred_element_type=F32)
    o_ref[...] = (acc[:hd] / acc[hd:hd + 1]).T.astype(o_ref.dtype)


def _fox_attn(qvt, k, caug, B, S, H, hd, tq, G=2):
    nq = S // tq
    tk = qvt.shape[-1]
    nk = S // tk
    n_sub = tq // tk
    hda = hd + AUG_ROWS
    ng = H // G
    return pl.pallas_call(
        _fox_attn_kernel,
        grid=(B, ng, nq),
        in_specs=[pl.BlockSpec((None, G, n_sub, hda, tk), lambda b, h, q: (b, h, q, 0, 0)),
                  pl.BlockSpec((None, G, S, hd), lambda b, h, q: (b, h, 0, 0)),
                  pl.BlockSpec((None, G, S, LANES), lambda b, h, q: (b, h, 0, 0)),
                  pl.BlockSpec((None, G, nk, hda, tk), lambda b, h, q: (b, ng + h, 0, 0, 0))],
        out_specs=pl.BlockSpec((tq, G * hd), lambda b, h, q: (b * nq + q, h)),
        out_shape=jax.ShapeDtypeStruct((B * S, H * hd), BF16),
        scratch_shapes=[pltpu.VMEM((tk, tq), F32), pltpu.VMEM((tk, tq), F32),
                        pltpu.VMEM((tk, tq), BF16), pltpu.VMEM((tk, tq), BF16),
                        pltpu.VMEM((1, tq), F32), pltpu.VMEM((hda, tq), F32)],
        compiler_params=_params(("parallel", "parallel", "arbitrary")),
        name="fox_attn",
    )(qvt, k, caug, qvt)


def kernel(x, c, ada_w, ada_b, ada_table, norm_mix_g, norm_ffn_g, lru_w_in, lru_conv_w, lru_conv_b, lru_w_a, lru_b_a, lru_w_x, lru_b_x, lru_lam, lru_w_out, fox_w_in, fox_b_f, fox_q_gain, fox_k_gain, fox_w_out, moe_w_router, moe_b_router, moe_w_gu, moe_b_gu, moe_w_down, moe_b_down):
    B, S, D = x.shape
    T = B * S
    depth = ada_table.shape[0]
    H = fox_b_f.shape[1]
    hd = D // H
    assert hd == LANES and ada_table.shape[1] == N_MOD

    m = _ada_proj(c, ada_w, ada_b)
    mod = (m[None] + ada_table.reshape(depth, 1, N_MOD * D)).reshape(depth, B, N_MOD, 1, D)
    x2 = x.reshape(T, D)

    for layer in range(depth):
        shift1, scale1, gate1, shift2, scale2, gate2 = (mod[layer, :, i] for i in range(N_MOD))
        g_mix = norm_mix_g[layer].reshape(1, D)
        g_ffn = norm_ffn_g[layer].reshape(1, D)
        j = layer // 2
        if layer % 2 == 0:
            nh, hb, _ = lru_w_a[j].shape
            assert hb % LANES == 0
            u = _norm_mm(x2, g_mix, shift1, scale1, lru_w_in[j].astype(BF16), S)
            w_ax = jnp.concatenate([lru_w_a[j], lru_w_x[j]], axis=-1).astype(BF16)
            b_ax = jnp.concatenate([lru_b_a[j].reshape(nh, 1, hb), lru_b_x[j].reshape(nh, 1, hb)], axis=-1)
            sp = jax.nn.softplus(-lru_lam[j]).reshape(1, nh * hb)
            y = _lru_core(u, lru_conv_w[j], lru_conv_b[j].reshape(1, -1), w_ax, b_ax, sp, B, S)
            w_out = lru_w_out[j].astype(BF16)
        else:
            tq = _tile(S, 512)
            w_in = fox_w_in[j]
            w_f = jnp.zeros((D, LANES), BF16).at[:, :H].set(w_in[:, 3 * D:].astype(BF16))
            qvt, k, f = _fox_proj(x2, g_mix, shift1, scale1, w_in.astype(BF16), w_f,
                                  fox_q_gain[j].reshape(1, hd), fox_k_gain[j].reshape(1, hd),
                                  B, S, H, hd, tq, tq // 2)
            caug = _cum_forget(f, fox_b_f[j], B, S, H)
            y = _fox_attn(qvt, k, caug, B, S, H, hd, tq)
            w_out = fox_w_out[j].astype(BF16)
        x2 = _out_proj(y, w_out, x2, gate1, S)
        x2 = _moe_layer(x2, g_ffn, shift2, scale2, gate2, moe_w_router[layer], moe_b_router[layer],
                        moe_w_gu, moe_b_gu, moe_w_down, moe_b_down, layer, S)
    return x2.reshape(B, S, D)
```

```python
import functools
import math

import jax
import jax.numpy as jnp
from jax import lax
from jax.experimental import pallas as pl
from jax.experimental.pallas import tpu as pltpu

F32 = jnp.float32
BF16 = jnp.bfloat16
I32 = jnp.int32
U32 = jnp.uint32

TOP_K = 4
N_MOD = 6
LRU_C = 8.0
SWIGLU_LIMIT = 7.0
SWIGLU_ALPHA = 1.702
EPS = 1e-6
LOG2E = 1.4426950408889634

AUG_ROWS = 16
LANES = 128
SUBLANES = 8
VMEM_LIMIT = 56 * 1024 * 1024
HIGHEST = lax.Precision.HIGHEST


def _params(sem, vmem=VMEM_LIMIT):
    return pltpu.CompilerParams(dimension_semantics=sem, vmem_limit_bytes=vmem)


def _tile(n, pref):
    t = min(n, pref)
    assert n % t == 0, (n, pref)
    return t


def _ada_kernel(ct_ref, w_ref, b_ref, o_ref, *, kc):
    K, nb = ct_ref.shape
    tn = w_ref.shape[1]

    def body(k, accs):
        r0 = pl.multiple_of(k * kc, kc)
        w = w_ref[pl.ds(r0, kc), :]
        c = ct_ref[pl.ds(r0, kc), :]
        s = c * jax.nn.sigmoid(c)
        return tuple(accs[b] + jnp.sum(w * s[:, b:b + 1], axis=0, keepdims=True) for b in range(nb))

    accs = lax.fori_loop(0, K // kc, body, tuple(jnp.zeros((1, tn), F32) for _ in range(nb)))
    for b in range(nb):
        o_ref[b:b + 1, :] = accs[b] + b_ref[...]


def _ada_proj(c, ada_w, ada_b):
    B, D = c.shape
    N = ada_w.shape[1]
    tn = _tile(N, 1024)
    kc = _tile(D, 256)
    return pl.pallas_call(
        functools.partial(_ada_kernel, kc=kc),
        grid=(N // tn,),
        in_specs=[pl.BlockSpec((D, B), lambda j: (0, 0)),
                  pl.BlockSpec((D, tn), lambda j: (0, j)),
                  pl.BlockSpec((1, tn), lambda j: (0, j))],
        out_specs=pl.BlockSpec((B, tn), lambda j: (0, j)),
        out_shape=jax.ShapeDtypeStruct((B, N), F32),
        compiler_params=_params(("parallel",)),
        name="ada_proj",
    )(c.T, ada_w, ada_b.reshape(1, N))


def _norm_mod_rows(x_ref, g_ref, sh_ref, sc_ref, emit, rc=64):
    tm = x_ref.shape[0]
    rc = min(rc, tm)
    gs = g_ref[...] * (1.0 + sc_ref[...])
    sh = sh_ref[...]

    def body(r, _):
        r0 = pl.multiple_of(r * rc, rc)
        x = x_ref[pl.ds(r0, rc), :]
        inv = lax.rsqrt(jnp.mean(x * x, axis=-1, keepdims=True) + EPS)
        emit(r0, rc, (x * inv) * gs + sh)
        return 0

    lax.fori_loop(0, tm // rc, body, 0)


def _norm_mm_kernel(x_ref, g_ref, sh_ref, sc_ref, w_ref, o_ref, h_ref):
    @pl.when(pl.program_id(1) == 0)
    def _():
        def emit(r0, rc, h):
            h_ref[pl.ds(r0, rc), :] = h.astype(BF16)
        _norm_mod_rows(x_ref, g_ref, sh_ref, sc_ref, emit)

    o_ref[...] = jnp.dot(h_ref[...], w_ref[...], preferred_element_type=F32).astype(o_ref.dtype)


def _norm_mm(x2, gain, shift, scale, w, S, tm=1024, tn=512):
    T, D = x2.shape
    N = w.shape[1]
    tm = _tile(S, tm)
    tn = _tile(N, tn)
    nt = S // tm
    return pl.pallas_call(
        _norm_mm_kernel,
        grid=(T // tm, N // tn),
        in_specs=[pl.BlockSpec((tm, D), lambda i, j: (i, 0)),
                  pl.BlockSpec((1, D), lambda i, j: (0, 0)),
                  pl.BlockSpec((None, 1, D), lambda i, j: (i // nt, 0, 0)),
                  pl.BlockSpec((None, 1, D), lambda i, j: (i // nt, 0, 0)),
                  pl.BlockSpec((D, tn), lambda i, j: (0, j))],
        out_specs=pl.BlockSpec((tm, tn), lambda i, j: (i, j)),
        out_shape=jax.ShapeDtypeStruct((T, N), BF16),
        scratch_shapes=[pltpu.VMEM((tm, D), BF16)],
        compiler_params=_params(("parallel", "arbitrary")),
        name="norm_mm",
    )(x2, gain, shift, scale, w)


def _gelu_tanh(x):
    c2 = 2.0 * math.sqrt(2.0 / math.pi)
    z2 = x * (c2 + (c2 * 0.044715) * (x * x))
    return x / (1.0 + jnp.exp(-z2))


def _lru_kernel(xb_ref, gb_ref, cw_ref, cb_ref, wax_ref, bax_ref, sp_ref, o_ref,
                xpad, a_s, b_s, hl_s, ac_s, carry):
    tt, C = xb_ref.shape
    L = tt // SUBLANES

    @pl.when(pl.program_id(2) == 0)
    def _():
        xpad[0:SUBLANES, :] = jnp.zeros((SUBLANES, C), F32)
        carry[...] = jnp.zeros_like(carry)

    xpad[SUBLANES:SUBLANES + tt, :] = xb_ref[...].astype(F32)
    cw = cw_ref[...]
    nk = cw.shape[0]
    xc = cb_ref[...] + cw[nk - 1:nk, :] * xpad[SUBLANES:SUBLANES + tt, :]
    for d in range(1, nk):
        xc = xc + cw[nk - 1 - d:nk - d, :] * xpad[SUBLANES - d:SUBLANES - d + tt, :]
    xpad[0:SUBLANES, :] = xpad[tt:tt + SUBLANES, :]

    ri = jnp.dot(xc.astype(BF16), wax_ref[...], preferred_element_type=F32) + bax_ref[...]
    r = jax.nn.sigmoid(ri[:, :C])
    ig = jax.nn.sigmoid(ri[:, C:])
    log_a = (-LRU_C) * r * sp_ref[...]
    a = jnp.exp(log_a)
    om = 1.0 - a * a
    mult = jnp.where(om > 0.0, om * lax.rsqrt(om), 0.0)
    nlv = C // LANES
    bt = mult * (ig * xc)
    for c in range(nlv):
        a_s[c] = a[:, c * LANES:(c + 1) * LANES]
        b_s[c] = bt[:, c * LANES:(c + 1) * LANES]

    def body(j, hA):
        j8 = pl.multiple_of(j * SUBLANES, SUBLANES)
        out = []
        for c in range(nlv):
            h, A = hA[c]
            aj = a_s[c, pl.ds(j, SUBLANES, stride=L), :]
            bj = b_s[c, pl.ds(j, SUBLANES, stride=L), :]
            h = aj * h + bj
            A = aj * A
            hl_s[c, pl.ds(j8, SUBLANES), :] = h
            ac_s[c, pl.ds(j8, SUBLANES), :] = A
            out.append((h, A))
        return tuple(out)

    init = tuple((jnp.zeros((SUBLANES, LANES), F32), jnp.ones((SUBLANES, LANES), F32)) for _ in range(nlv))
    ends = lax.fori_loop(0, L, body, init, unroll=min(L, 8))

    for c in range(nlv):
        h_end, a_end = ends[c]
        cs = slice(c * LANES, (c + 1) * LANES)
        e = carry[0:1, cs]
        for s in range(SUBLANES):
            hl = hl_s[c, pl.ds(s, L, stride=SUBLANES), :]
            ac = ac_s[c, pl.ds(s, L, stride=SUBLANES), :]
            hs = hl + ac * e
            gate = _gelu_tanh(gb_ref[s * L:(s + 1) * L, cs].astype(F32))
            o_ref[s * L:(s + 1) * L, cs] = (hs * gate).astype(o_ref.dtype)
            e = h_end[s:s + 1, :] + a_end[s:s + 1, :] * e
        carry[0:1, cs] = e


def _lru_core(u, conv_w, conv_b, w_ax, b_ax, softplus_neg_lam, B, S, tt=2048):
    T, C2 = u.shape
    C = C2 // 2
    nh, hb, _ = w_ax.shape
    tt = _tile(S, tt)
    nt = S // tt
    return pl.pallas_call(
        _lru_kernel,
        grid=(B, nh, nt),
        in_specs=[pl.BlockSpec((tt, hb), lambda b, h, t: (b * nt + t, h)),
                  pl.BlockSpec((tt, hb), lambda b, h, t: (b * nt + t, nh + h)),
                  pl.BlockSpec((conv_w.shape[0], hb), lambda b, h, t: (0, h)),
                  pl.BlockSpec((1, hb), lambda b, h, t: (0, h)),
                  pl.BlockSpec((None, hb, 2 * hb), lambda b, h, t: (h, 0, 0)),
                  pl.BlockSpec((None, 1, 2 * hb), lambda b, h, t: (h, 0, 0)),
                  pl.BlockSpec((1, hb), lambda b, h, t: (0, h))],
        out_specs=pl.BlockSpec((tt, hb), lambda b, h, t: (b * nt + t, h)),
        out_shape=jax.ShapeDtypeStruct((T, C), BF16),
        scratch_shapes=[pltpu.VMEM((tt + SUBLANES, hb), F32),
                        pltpu.VMEM((hb // LANES, tt, LANES), F32), pltpu.VMEM((hb // LANES, tt, LANES), F32),
                        pltpu.VMEM((hb // LANES, tt, LANES), F32), pltpu.VMEM((hb // LANES, tt, LANES), F32),
                        pltpu.VMEM((SUBLANES, hb), F32)],
        compiler_params=_params(("parallel", "parallel", "arbitrary")),
        name="lru_core",
    )(u, u, conv_w, conv_b, w_ax, b_ax, softplus_neg_lam)


def _out_proj_kernel(y_ref, w_ref, x_ref, g_ref, o_ref):
    o_ref[...] = x_ref[...] + g_ref[...] * jnp.dot(y_ref[...], w_ref[...], preferred_element_type=F32)


def _out_proj(y, w, x2, gate, S, tm=1024, tn=1024):
    T, D = x2.shape
    K = y.shape[1]
    tm = _tile(S, tm)
    tn = _tile(D, tn)
    nt = S // tm
    return pl.pallas_call(
        _out_proj_kernel,
        grid=(T // tm, D // tn),
        in_specs=[pl.BlockSpec((tm, K), lambda i, j: (i, 0)),
                  pl.BlockSpec((K, tn), lambda i, j: (0, j)),
                  pl.BlockSpec((tm, tn), lambda i, j: (i, j)),
                  pl.BlockSpec((None, 1, tn), lambda i, j: (i // nt, 0, j))],
        out_specs=pl.BlockSpec((tm, tn), lambda i, j: (i, j)),
        out_shape=jax.ShapeDtypeStruct((T, D), F32),
        compiler_params=_params(("parallel", "parallel")),
        name="out_proj",
    )(y, w, x2, gate)


def _pack_bf16_pair(lo, hi):
    ul = lax.bitcast_convert_type(lo, U32)
    uh = lax.bitcast_convert_type(hi, U32)
    rl = (ul + 0x8000) >> 16
    rh = ((uh + 0x8000) >> 16) << 16
    return rh | rl


def _unpack_bf16_pair(w):
    lo = lax.bitcast_convert_type(w << 16, F32)
    hi = lax.bitcast_convert_type((w >> 16) << 16, F32)
    return lo, hi


def _unpack_rows_bf16(w):
    lo, hi = _unpack_bf16_pair(w)
    return jnp.concatenate([lo.astype(BF16), hi.astype(BF16)], axis=1)


def _router_kernel(x_ref, g_ref, sh_ref, sc_ref, wr_ref, br_ref,
                   hp_ref, idx_ref, gate_ref, rank_ref, cnt_ref, carry, lg_ref, *, n_exp):
    tm, D = x_ref.shape

    @pl.when(pl.program_id(0) == 0)
    def _():
        carry[...] = jnp.zeros_like(carry)

    def emit(r0, rc, h):
        hp_ref[pl.ds(r0, rc), :] = _pack_bf16_pair(h[:, :D // 2], h[:, D // 2:])
        lg_ref[pl.ds(r0, rc), :] = jnp.dot(h.astype(BF16), wr_ref[...], preferred_element_type=F32)
    _norm_mod_rows(x_ref, g_ref, sh_ref, sc_ref, emit)

    logits = lg_ref[...] + br_ref[...]
    v = logits.T[:n_exp, :]
    eio = lax.broadcasted_iota(I32, (n_exp, tm), 0)
    vals, idxs = [], []
    for _ in range(TOP_K):
        m = jnp.max(v, axis=0, keepdims=True)
        ik = jnp.min(jnp.where(v == m, eio, n_exp), axis=0, keepdims=True)
        vals.append(m)
        idxs.append(ik)
        v = jnp.where(eio == ik, -jnp.inf, v)
    ex = [jnp.exp(vk - vals[0]) for vk in vals]
    den = ex[0] + ex[1] + ex[2] + ex[3]

    onehot = jnp.zeros((n_exp, tm), F32)
    for ik in idxs:
        onehot = onehot + (eio == ik).astype(F32)
    si = lax.broadcasted_iota(I32, (tm, tm), 0)
    ti = lax.broadcasted_iota(I32, (tm, tm), 1)
    before = (si < ti).astype(BF16)
    base = jnp.dot(onehot.astype(BF16), before, preferred_element_type=F32) + carry[:, 0:1]

    zi = jnp.zeros((SUBLANES - TOP_K, tm), I32)
    idx_ref[...] = jnp.concatenate(idxs + [zi], axis=0)
    gate_ref[...] = jnp.concatenate([e / den for e in ex] + [zi.astype(F32)], axis=0)
    ranks = [jnp.sum(jnp.where(eio == ik, base, 0.0), axis=0, keepdims=True).astype(I32) for ik in idxs]
    rank_ref[...] = jnp.concatenate(ranks + [zi], axis=0)

    carry[...] = carry[...] + jnp.sum(onehot, axis=1, keepdims=True)
    cnt_ref[...] = carry[...]


def _router(x2, gain, shift, scale, w_r, b_r, S, tm=512):
    T, D = x2.shape
    E = w_r.shape[1]
    tm = _tile(S, tm)
    nt = S // tm
    wr = jnp.zeros((D, LANES), BF16).at[:, :E].set(w_r.astype(BF16))
    br = jnp.zeros((1, LANES), F32).at[0, :E].set(b_r)
    return pl.pallas_call(
        functools.partial(_router_kernel, n_exp=E),
        grid=(T // tm,),
        in_specs=[pl.BlockSpec((tm, D), lambda i: (i, 0)),
                  pl.BlockSpec((1, D), lambda i: (0, 0)),
                  pl.BlockSpec((None, 1, D), lambda i: (i // nt, 0, 0)),
                  pl.BlockSpec((None, 1, D), lambda i: (i // nt, 0, 0)),
                  pl.BlockSpec((D, LANES), lambda i: (0, 0)),
                  pl.BlockSpec((1, LANES), lambda i: (0, 0))],
        out_specs=[pl.BlockSpec((tm, D // 2), lambda i: (i, 0)),
                   pl.BlockSpec((SUBLANES, tm), lambda i: (0, i)),
                   pl.BlockSpec((SUBLANES, tm), lambda i: (0, i)),
                   pl.BlockSpec((SUBLANES, tm), lambda i: (0, i)),
                   pl.BlockSpec((E, LANES), lambda i: (0, 0))],
        out_shape=[jax.ShapeDtypeStruct((T, D // 2), U32),
                   jax.ShapeDtypeStruct((SUBLANES, T), I32),
                   jax.ShapeDtypeStruct((SUBLANES, T), F32),
                   jax.ShapeDtypeStruct((SUBLANES, T), I32),
                   jax.ShapeDtypeStruct((E, LANES), F32)],
        scratch_shapes=[pltpu.VMEM((E, LANES), F32), pltpu.VMEM((tm, LANES), F32)],
        compiler_params=_params(("arbitrary",)),
        name="moe_router",
    )(x2, gain, shift, scale, wr, br)


def _dispatch_kernel(dest_ref, cnt_ref, pst_ref, pad_ref, nu_ref, h_ref, xs_ref, zblk, sem):
    tc = h_ref.shape[0]
    tm_e = zblk.shape[0]
    i = pl.program_id(0)
    n_tok = pl.num_programs(0) * tc
    n_exp = cnt_ref.shape[0]
    n_blk = xs_ref.shape[0] // tm_e

    def copy(t, k):
        d = dest_ref[k * n_tok + i * tc + t]
        return pltpu.make_async_copy(h_ref.at[pl.ds(t, 1)], xs_ref.at[pl.ds(d, 1)], sem.at[0])

    def pad_copy(r):
        return pltpu.make_async_copy(zblk.at[pl.ds(0, 1)], xs_ref.at[pl.ds(r, 1)], sem.at[1])

    def blk_copy(b):
        r0 = pl.multiple_of(b * tm_e, tm_e)
        return pltpu.make_async_copy(zblk, xs_ref.at[pl.ds(r0, tm_e)], sem.at[2])

    def for_pad_rows(fn):
        def per_expert(e, _):
            def per_row(r, _):
                fn(pad_copy(r))
                return 0
            lax.fori_loop(pst_ref[e] + cnt_ref[e], pst_ref[e] + pad_ref[e], per_row, 0)
            return 0
        lax.fori_loop(0, n_exp, per_expert, 0)

        def per_blk(b, _):
            fn(blk_copy(b))
            return 0
        lax.fori_loop(nu_ref[0], n_blk, per_blk, 0)

    @pl.when(i == 0)
    def _():
        zblk[...] = jnp.zeros(zblk.shape, zblk.dtype)
        for_pad_rows(lambda c: c.start())

    def start(t, _):
        for k in range(TOP_K):
            copy(t, k).start()
        return 0

    def wait(t, _):
        for k in range(TOP_K):
            copy(t, k).wait()
        return 0

    lax.fori_loop(0, tc, start, 0)
    lax.fori_loop(0, tc, wait, 0)

    @pl.when(i == 0)
    def _():
        for_pad_rows(lambda c: c.wait())


def _dispatch(dest_flat, counts, pstart, padded, n_used, h, n_rows, tm_e, tc=256):
    T, D = h.shape
    tc = _tile(T, tc)
    return pl.pallas_call(
        _dispatch_kernel,
        grid_spec=pltpu.PrefetchScalarGridSpec(
            num_scalar_prefetch=5,
            grid=(T // tc,),
            in_specs=[pl.BlockSpec((tc, D), lambda i, *_: (i, 0))],
            out_specs=pl.BlockSpec(memory_space=pl.ANY),
            scratch_shapes=[pltpu.VMEM((tm_e, D), h.dtype), pltpu.SemaphoreType.DMA((3,))]),
        out_shape=jax.ShapeDtypeStruct((n_rows, D), h.dtype),
        compiler_params=_params(("arbitrary",)),
        name="moe_dispatch",
    )(dest_flat, counts, pstart, padded, n_used, h)


def _refresh_expert_weight(be_ref, w_ref, wbf_ref, rows=256):
    i = pl.program_id(0)
    changed = (i == 0) | (be_ref[i] != be_ref[jnp.maximum(i - 1, 0)])

    @pl.when(changed)
    def _():
        rc = min(rows, w_ref.shape[0])

        def body(r, _):
            r0 = pl.multiple_of(r * rc, rc)
            wbf_ref[pl.ds(r0, rc), :] = w_ref[pl.ds(r0, rc), :].astype(BF16)
            return 0

        lax.fori_loop(0, w_ref.shape[0] // rc, body, 0)


def _expert_gate_kernel(be_ref, nu_ref, xs_ref, w_ref, b_ref, o_ref, wbf_ref):
    _refresh_expert_weight(be_ref, w_ref, wbf_ref)
    i = pl.program_id(0)

    @pl.when(i < nu_ref[0])
    def _():
        g = jnp.dot(_unpack_rows_bf16(xs_ref[...]), wbf_ref[...], preferred_element_type=F32) + b_ref[...]
        g = jnp.minimum(g, SWIGLU_LIMIT)
        o_ref[...] = (g * jax.nn.sigmoid(SWIGLU_ALPHA * g)).astype(o_ref.dtype)

    @pl.when(i >= nu_ref[0])
    def _():
        o_ref[...] = jnp.zeros_like(o_ref)


def _expert_up_kernel(be_ref, nu_ref, xs_ref, w_ref, b_ref, glu_ref, o_ref, wbf_ref):
    _refresh_expert_weight(be_ref, w_ref, wbf_ref)
    i = pl.program_id(0)

    @pl.when(i < nu_ref[0])
    def _():
        up = jnp.dot(_unpack_rows_bf16(xs_ref[...]), wbf_ref[...], preferred_element_type=F32) + b_ref[...]
        up = jnp.clip(up, -SWIGLU_LIMIT, SWIGLU_LIMIT)
        o_ref[...] = ((up + 1.0) * glu_ref[...].astype(F32)).astype(o_ref.dtype)

    @pl.when(i >= nu_ref[0])
    def _():
        o_ref[...] = jnp.zeros_like(o_ref)


def _expert_down_kernel(be_ref, nu_ref, a_ref, w_ref, b_ref, o_ref, wbf_ref):
    _refresh_expert_weight(be_ref, w_ref, wbf_ref)
    i = pl.program_id(0)

    @pl.when(i < nu_ref[0])
    def _():
        y = jnp.dot(a_ref[...], wbf_ref[...], preferred_element_type=F32) + b_ref[...]
        dh = y.shape[1] // 2
        o_ref[...] = _pack_bf16_pair(y[:, :dh], y[:, dh:])

    @pl.when(i >= nu_ref[0])
    def _():
        o_ref[...] = jnp.zeros_like(o_ref)


def _experts(blk_expert, n_used, xs, w_gu, b_gu, w_down, b_down, layer, tm):
    n_rows, Dh = xs.shape
    _, E, D, F2 = w_gu.shape
    F = F2 // 2
    n_blk = n_rows // tm
    b_gu3 = b_gu[layer].reshape(E, 1, F2)

    def row_idx(i, be, nu):
        return (jnp.minimum(i, nu[0] - 1), 0)

    def call(kern, name, half, extra_in, extra_specs):
        return pl.pallas_call(
            kern,
            grid_spec=pltpu.PrefetchScalarGridSpec(
                num_scalar_prefetch=2,
                grid=(n_blk,),
                in_specs=[pl.BlockSpec((tm, Dh), row_idx),
                          pl.BlockSpec((None, None, D, F), lambda i, be, nu: (layer, be[i], 0, half)),
                          pl.BlockSpec((None, 1, F), lambda i, be, nu: (be[i], 0, half))] + extra_specs,
                out_specs=pl.BlockSpec((tm, F), lambda i, be, nu: (i, 0)),
                scratch_shapes=[pltpu.VMEM((D, F), BF16)]),
            out_shape=jax.ShapeDtypeStruct((n_rows, F), BF16),
            compiler_params=_params(("arbitrary",)),
            name=name,
        )(blk_expert, n_used, xs, w_gu, b_gu3, *extra_in)

    glu = call(_expert_gate_kernel, "moe_gate", 0, [], [])
    act = call(_expert_up_kernel, "moe_up", 1, [glu], [pl.BlockSpec((tm, F), row_idx)])

    return pl.pallas_call(
        _expert_down_kernel,
        grid_spec=pltpu.PrefetchScalarGridSpec(
            num_scalar_prefetch=2,
            grid=(n_blk,),
            in_specs=[pl.BlockSpec((tm, F), row_idx),
                      pl.BlockSpec((None, None, F, D), lambda i, be, nu: (layer, be[i], 0, 0)),
                      pl.BlockSpec((None, 1, D), lambda i, be, nu: (be[i], 0, 0))],
            out_specs=pl.BlockSpec((tm, Dh), lambda i, be, nu: (i, 0)),
            scratch_shapes=[pltpu.VMEM((F, D), BF16)]),
        out_shape=jax.ShapeDtypeStruct((n_rows, Dh), U32),
        compiler_params=_params(("arbitrary",)),
        name="moe_down",
    )(blk_expert, n_used, act, w_down, b_down[layer].reshape(E, 1, D))


def _combine_kernel(dest_ref, ys_ref, x_ref, g5_ref, gt_ref, o_ref, buf, gcol, sem):
    tc, D = x_ref.shape
    dh = D // 2
    lc = dh // SUBLANES
    i = pl.program_id(0)
    n = pl.num_programs(0)
    n_tok = n * tc

    def copy(tile, slot, t, k):
        d = dest_ref[k * n_tok + tile * tc + t]
        return pltpu.make_async_copy(ys_ref.at[pl.ds(d, 1)], buf.at[slot, k, pl.ds(t, 1)], sem.at[slot])

    def start_tile(tile, slot):
        def body(t, _):
            for k in range(TOP_K):
                copy(tile, slot, t, k).start()
            return 0
        lax.fori_loop(0, tc, body, 0)

    def wait_tile(tile, slot):
        def body(t, _):
            for k in range(TOP_K):
                copy(tile, slot, t, k).wait()
            return 0
        lax.fori_loop(0, tc, body, 0)

    slot = i % 2

    @pl.when(i == 0)
    def _():
        start_tile(0, 0)

    wait_tile(i, slot)
    diag = lax.broadcasted_iota(I32, (tc, tc), 0) == lax.broadcasted_iota(I32, (tc, tc), 1)
    for k in range(TOP_K):
        col = jnp.sum(jnp.where(diag, gt_ref[k:k + 1, :], 0.0), axis=1, keepdims=True)
        gcol[k] = jnp.broadcast_to(col, (tc, LANES))

    def group(g, issue_next):
        r0 = pl.multiple_of(g * SUBLANES, SUBLANES)
        rows = pl.ds(r0, SUBLANES)
        cb = []
        for k in range(TOP_K):
            v = gcol[k, rows, :]
            cb.append(v[:, :lc] if lc <= LANES else jnp.concatenate([v] * (lc // LANES), axis=1))
        for t in range(SUBLANES):
            if issue_next:
                for k in range(TOP_K):
                    copy(i + 1, 1 - slot, r0 + t, k).start()
            c0 = t * lc
            acc_lo = acc_hi = None
            for k in range(TOP_K):
                lo, hi = _unpack_bf16_pair(buf[slot, k, rows, c0:c0 + lc])
                acc_lo = cb[k] * lo if acc_lo is None else acc_lo + cb[k] * lo
                acc_hi = cb[k] * hi if acc_hi is None else acc_hi + cb[k] * hi
            o_ref[rows, c0:c0 + lc] = x_ref[rows, c0:c0 + lc] + g5_ref[:, c0:c0 + lc] * acc_lo
            o_ref[rows, dh + c0:dh + c0 + lc] = (x_ref[rows, dh + c0:dh + c0 + lc]
                                                 + g5_ref[:, dh + c0:dh + c0 + lc] * acc_hi)
        return 0

    @pl.when(i + 1 < n)
    def _():
        lax.fori_loop(0, tc // SUBLANES, lambda g, _: group(g, True), 0)

    @pl.when(i + 1 >= n)
    def _():
        lax.fori_loop(0, tc // SUBLANES, lambda g, _: group(g, False), 0)


def _combine(dest_flat, ys, x2, gate5, gates, S, tc=128):
    T, D = x2.shape
    tc = _tile(S, tc)
    nt = S // tc
    return pl.pallas_call(
        _combine_kernel,
        grid_spec=pltpu.PrefetchScalarGridSpec(
            num_scalar_prefetch=1,
            grid=(T // tc,),
            in_specs=[pl.BlockSpec(memory_space=pl.ANY),
                      pl.BlockSpec((tc, D), lambda i, d: (i, 0)),
                      pl.BlockSpec((None, 1, D), lambda i, d: (i // nt, 0, 0)),
                      pl.BlockSpec((SUBLANES, tc), lambda i, d: (0, i))],
            out_specs=pl.BlockSpec((tc, D), lambda i, d: (i, 0)),
            scratch_shapes=[pltpu.VMEM((2, TOP_K, tc, D // 2), ys.dtype),
                            pltpu.VMEM((TOP_K, tc, LANES), F32),
                            pltpu.SemaphoreType.DMA((2,))]),
        out_shape=jax.ShapeDtypeStruct((T, D), F32),
        compiler_params=_params(("arbitrary",)),
        name="moe_combine",
    )(dest_flat, ys, x2, gate5, gates)


def _moe_layer(x2, gain, shift, scale, gate5, w_r, b_r, w_gu, b_gu, w_down, b_down, layer, S, tm_e=512):
    T, D = x2.shape
    E = w_r.shape[1]
    h, idx, gates, rank, cnt = _router(x2, gain, shift, scale, w_r, b_r, S)
    counts = cnt[:, 0].astype(I32)
    padded = (counts + tm_e - 1) // tm_e * tm_e
    pend = jnp.cumsum(padded)
    pstart = pend - padded
    onehot = idx[:TOP_K, :, None] == jnp.arange(E, dtype=I32)
    dest = (jnp.sum(jnp.where(onehot, pstart, 0), axis=-1) + rank[:TOP_K]).reshape(TOP_K * T)
    n_rows = (T * TOP_K // tm_e + E) * tm_e
    n_blk = n_rows // tm_e
    n_used = (pend[E - 1] // tm_e).astype(I32)
    bi = jnp.minimum(jnp.arange(n_blk, dtype=I32), n_used - 1)
    blk_expert = jnp.minimum(jnp.sum(pend[None, :] <= (bi * tm_e)[:, None], axis=1), E - 1).astype(I32)

    n_used = n_used.reshape(1)
    xs = _dispatch(dest, counts, pstart, padded, n_used, h, n_rows, tm_e)
    ys = _experts(blk_expert, n_used, xs, w_gu, b_gu, w_down, b_down, layer, tm_e)
    return _combine(dest, ys, x2, gate5, gates, S)


def _fox_qkv_kernel(x_ref, g_ref, sh_ref, sc_ref, w_ref, wf_ref, gq_ref, gk_ref, ot_ref, ok_ref, f_ref, h_ref,
                    *, n_q_tiles, hd, q_scale):
    j = pl.program_id(1)
    hpt, n_sub, _, tk = ot_ref.shape

    @pl.when(j == 0)
    def _():
        def emit(r0, rc, h):
            h_ref[pl.ds(r0, rc), :] = h.astype(BF16)
        _norm_mod_rows(x_ref, g_ref, sh_ref, sc_ref, emit)
        f_ref[...] = jnp.dot(h_ref[...], wf_ref[...], preferred_element_type=F32)

    def head_norm(yh, gain):
        inv = lax.rsqrt(jnp.mean(yh * yh, axis=-1, keepdims=True) + EPS)
        return (yh * inv) * gain

    def chunks(epilogue):
        for s in range(n_sub):
            y = jnp.dot(h_ref[s * tk:(s + 1) * tk, :], w_ref[...], preferred_element_type=F32)
            for hh in range(hpt):
                epilogue(hh, s, y[:, hh * hd:(hh + 1) * hd])

    aug_row = lax.broadcasted_iota(I32, (AUG_ROWS, tk), 0)
    q_aug = jnp.where(aug_row < 3, -1.0, 0.0).astype(ot_ref.dtype)
    v_aug = jnp.where(aug_row < 1, 1.0, 0.0).astype(ot_ref.dtype)

    @pl.when(j < n_q_tiles)
    def _():
        gain = gq_ref[...] * q_scale

        def epilogue(hh, s, yh):
            ot_ref[hh, s, 0:hd, :] = head_norm(yh, gain).T.astype(ot_ref.dtype)
            ot_ref[hh, s, hd:hd + AUG_ROWS, :] = q_aug
        chunks(epilogue)

    @pl.when((j >= n_q_tiles) & (j < 2 * n_q_tiles))
    def _():
        def epilogue(hh, s, yh):
            ok_ref[hh, s * tk:(s + 1) * tk, :] = head_norm(yh, gk_ref[...]).astype(ok_ref.dtype)
        chunks(epilogue)

    @pl.when(j >= 2 * n_q_tiles)
    def _():
        def epilogue(hh, s, yh):
            ot_ref[hh, s, 0:hd, :] = yh.T.astype(ot_ref.dtype)
            ot_ref[hh, s, hd:hd + AUG_ROWS, :] = v_aug
        chunks(epilogue)


def _fox_proj(x2, gain, shift, scale, w_in, w_f, q_gain, k_gain, B, S, H, hd, tm, tk, hpt=8):
    T, D = x2.shape
    hpt = _tile(H, hpt)
    tn = hpt * hd
    nt = S // tm
    n_sub = tm // tk
    n_q_tiles = H // hpt
    q_scale = (hd ** -0.5) * LOG2E

    nqt = n_q_tiles

    def t_idx(j):
        return jnp.where(j < nqt, j, jnp.where(j < 2 * nqt, nqt - 1, j - nqt))

    def k_idx(j):
        return jnp.clip(j - nqt, 0, nqt - 1)

    qvt, k, f = pl.pallas_call(
        functools.partial(_fox_qkv_kernel, n_q_tiles=nqt, hd=hd, q_scale=q_scale),
        grid=(T // tm, 3 * nqt),
        in_specs=[pl.BlockSpec((tm, D), lambda i, j: (i, 0)),
                  pl.BlockSpec((1, D), lambda i, j: (0, 0)),
                  pl.BlockSpec((None, 1, D), lambda i, j: (i // nt, 0, 0)),
                  pl.BlockSpec((None, 1, D), lambda i, j: (i // nt, 0, 0)),
                  pl.BlockSpec((D, tn), lambda i, j: (0, j)),
                  pl.BlockSpec((D, LANES), lambda i, j: (0, 0)),
                  pl.BlockSpec((1, hd), lambda i, j: (0, 0)),
                  pl.BlockSpec((1, hd), lambda i, j: (0, 0))],
        out_specs=[pl.BlockSpec((None, hpt, n_sub, hd + AUG_ROWS, tk),
                                lambda i, j: (i // nt, t_idx(j), i % nt, 0, 0)),
                   pl.BlockSpec((None, hpt, tm, hd), lambda i, j: (i // nt, k_idx(j), i % nt, 0)),
                   pl.BlockSpec((tm, LANES), lambda i, j: (i, 0))],
        out_shape=[jax.ShapeDtypeStruct((B, 2 * H, S // tk, hd + AUG_ROWS, tk), BF16),
                   jax.ShapeDtypeStruct((B, H, S, hd), BF16),
                   jax.ShapeDtypeStruct((T, LANES), F32)],
        scratch_shapes=[pltpu.VMEM((tm, D), BF16)],
        compiler_params=_params(("parallel", "arbitrary")),
        name="fox_qkv_proj",
    )(x2, gain, shift, scale, w_in, w_f, q_gain, k_gain)
    return qvt, k, f


def _cum_kernel(f_ref, bf_ref, o_ref, carry):
    tt = f_ref.shape[0]

    @pl.when(pl.program_id(1) == 0)
    def _():
        carry[...] = jnp.zeros_like(carry)

    logf = jax.nn.log_sigmoid(f_ref[...] + bf_ref[...])
    si = lax.broadcasted_iota(I32, (tt, tt), 0)
    ti = lax.broadcasted_iota(I32, (tt, tt), 1)
    incl = (ti <= si).astype(F32)
    c = jnp.dot(incl, logf, preferred_element_type=F32, precision=HIGHEST) + carry[0:1, :]
    carry[0:1, :] = c[tt - 1:tt, :]
    c2 = c * LOG2E
    hi = c2.astype(BF16).astype(F32)
    r1 = c2 - hi
    mid = r1.astype(BF16).astype(F32)
    lo = (r1 - mid).astype(BF16).astype(F32)
    lane = lax.broadcasted_iota(I32, (tt, LANES), 1)
    for h in range(o_ref.shape[0]):
        t = jnp.where(lane == 0, hi[:, h:h + 1],
                      jnp.where(lane == 1, mid[:, h:h + 1],
                                jnp.where(lane == 2, lo[:, h:h + 1], 0.0)))
        o_ref[h] = t.astype(o_ref.dtype)


def _cum_forget(f, b_f, B, S, H, tt=256):
    T = f.shape[0]
    tt = _tile(S, tt)
    nt = S // tt
    bf = jnp.zeros((1, LANES), F32).at[0, :H].set(b_f)
    return pl.pallas_call(
        _cum_kernel,
        grid=(B, nt),
        in_specs=[pl.BlockSpec((tt, LANES), lambda b, t: (b * nt + t, 0)),
                  pl.BlockSpec((1, LANES), lambda b, t: (0, 0))],
        out_specs=pl.BlockSpec((None, H, tt, LANES), lambda b, t: (b, 0, t, 0)),
        out_shape=jax.ShapeDtypeStruct((B, H, S, LANES), BF16),
        scratch_shapes=[pltpu.VMEM((SUBLANES, LANES), F32)],
        compiler_params=_params(("parallel", "arbitrary")),
        name="fox_cum",
    )(f, bf)


def _fox_attn_kernel(qt_ref, k_ref, c_ref, vt_ref, o_ref, *scratch):
    hd = k_ref.shape[2]
    for g in range(qt_ref.shape[0]):
        _fox_attn_head(qt_ref.at[g], k_ref.at[g], c_ref.at[g], vt_ref.at[g],
                       o_ref.at[:, g * hd:(g + 1) * hd], *scratch)


def _fox_attn_head(qt_ref, k_ref, c_ref, vt_ref, o_ref, s_a, s_b, p_a, p_b, m_ref, acc_ref):
    n_sub, hda, tk = qt_ref.shape
    hd = hda - AUG_ROWS
    tq = n_sub * tk
    assert n_sub == 2
    qi = pl.program_id(2)
    qt = jnp.concatenate([qt_ref[s] for s in range(n_sub)], axis=1)
    qp = jnp.concatenate([qt, jnp.zeros((hd - AUG_ROWS, tq), BF16)], axis=0)

    def scores(j):
        r0 = pl.multiple_of(j * tk, tk)
        kp = jnp.concatenate([k_ref[pl.ds(r0, tk), :], c_ref[pl.ds(r0, tk), :]], axis=1)
        return jnp.dot(kp, qp, preferred_element_type=F32)

    def stage(j, s_cur, s_nxt, p_prev, p_cur, mask=None):
        if s_nxt is not None:
            s_nxt[...] = scores(j + 1)
        pv = jnp.dot(vt_ref[jnp.maximum(j - 1, 0)], p_prev[...], preferred_element_type=F32)
        t = s_cur[...]
        if mask is not None:
            t = jnp.where(mask, t, -jnp.inf)
        m_old = m_ref[...]
        m_new = jnp.maximum(m_old, jnp.max(t, axis=0, keepdims=True))
        alpha = jnp.exp2(m_old - m_new)
        p_cur[...] = jnp.exp2(t - m_new).astype(BF16)
        acc_ref[...] = alpha * (acc_ref[...] + pv)
        m_ref[...] = m_new

    m_ref[...] = jnp.full(m_ref.shape, -jnp.inf, F32)
    acc_ref[...] = jnp.zeros(acc_ref.shape, F32)
    p_b[...] = jnp.zeros(p_b.shape, BF16)
    s_a[...] = scores(0)

    def pair(p):
        stage(2 * p, s_a, s_b, p_b, p_a)
        stage(2 * p + 1, s_b, s_a, p_a, p_b)

    def quad(i, _):
        pair(2 * i)
        pair(2 * i + 1)
        return 0

    lax.fori_loop(0, qi // 2, quad, 0)

    @pl.when(qi % 2 == 1)
    def _():
        pair(qi - 1)

    key = lax.broadcasted_iota(I32, (tk, tq), 0)
    qry = lax.broadcasted_iota(I32, (tk, tq), 1)
    stage(2 * qi, s_a, s_b, p_b, p_a, mask=key <= qry)
    stage(2 * qi + 1, s_b, None, p_a, p_b, mask=key + tk <= qry)
    acc = acc_ref[...] + jnp.dot(vt_ref[2 * qi + 1], p_b[...], preferred_element_type=F32)
    o_ref[...] = (acc[:hd] / acc[hd:hd + 1]).T.astype(o_ref.dtype)


def _fox_attn(qvt, k, caug, B, S, H, hd, tq, G=2):
    nq = S // tq
    tk = qvt.shape[-1]
    nk = S // tk
    n_sub = tq // tk
    hda = hd + AUG_ROWS
    ng = H // G
    return pl.pallas_call(
        _fox_attn_kernel,
        grid=(B, ng, nq),
        in_specs=[pl.BlockSpec((None, G, n_sub, hda, tk), lambda b, h, q: (b, h, q, 0, 0)),
                  pl.BlockSpec((None, G, S, hd), lambda b, h, q: (b, h, 0, 0)),
                  pl.BlockSpec((None, G, S, LANES), lambda b, h, q: (b, h, 0, 0)),
                  pl.BlockSpec((None, G, nk, hda, tk), lambda b, h, q: (b, ng + h, 0, 0, 0))],
        out_specs=pl.BlockSpec((tq, G * hd), lambda b, h, q: (b * nq + q, h)),
        out_shape=jax.ShapeDtypeStruct((B * S, H * hd), BF16),
        scratch_shapes=[pltpu.VMEM((tk, tq), F32), pltpu.VMEM((tk, tq), F32),
                        pltpu.VMEM((tk, tq), BF16), pltpu.VMEM((tk, tq), BF16),
                        pltpu.VMEM((1, tq), F32), pltpu.VMEM((hda, tq), F32)],
        compiler_params=_params(("parallel", "parallel", "arbitrary")),
        name="fox_attn",
    )(qvt, k, caug, qvt)


def kernel(x, c, ada_w, ada_b, ada_table, norm_mix_g, norm_ffn_g, lru_w_in, lru_conv_w, lru_conv_b, lru_w_a, lru_b_a, lru_w_x, lru_b_x, lru_lam, lru_w_out, fox_w_in, fox_b_f, fox_q_gain, fox_k_gain, fox_w_out, moe_w_router, moe_b_router, moe_w_gu, moe_b_gu, moe_w_down, moe_b_down):
    B, S, D = x.shape
    T = B * S
    depth = ada_table.shape[0]
    H = fox_b_f.shape[1]
    hd = D // H
    assert hd == LANES and ada_table.shape[1] == N_MOD

    m = _ada_proj(c, ada_w, ada_b)
    mod = (m[None] + ada_table.reshape(depth, 1, N_MOD * D)).reshape(depth, B, N_MOD, 1, D)
    x2 = x.reshape(T, D)

    for layer in range(depth):
        shift1, scale1, gate1, shift2, scale2, gate2 = (mod[layer, :, i] for i in range(N_MOD))
        g_mix = norm_mix_g[layer].reshape(1, D)
        g_ffn = norm_ffn_g[layer].reshape(1, D)
        j = layer // 2
        if layer % 2 == 0:
            nh, hb, _ = lru_w_a[j].shape
            assert hb % LANES == 0
            u = _norm_mm(x2, g_mix, shift1, scale1, lru_w_in[j].astype(BF16), S)
            w_ax = jnp.concatenate([lru_w_a[j], lru_w_x[j]], axis=-1).astype(BF16)
            b_ax = jnp.concatenate([lru_b_a[j].reshape(nh, 1, hb), lru_b_x[j].reshape(nh, 1, hb)], axis=-1)
            sp = jax.nn.softplus(-lru_lam[j]).reshape(1, nh * hb)
            y = _lru_core(u, lru_conv_w[j], lru_conv_b[j].reshape(1, -1), w_ax, b_ax, sp, B, S)
            w_out = lru_w_out[j].astype(BF16)
        else:
            tq = _tile(S, 512)
            w_in = fox_w_in[j]
            w_f = jnp.zeros((D, LANES), BF16).at[:, :H].set(w_in[:, 3 * D:].astype(BF16))
            qvt, k, f = _fox_proj(x2, g_mix, shift1, scale1, w_in.astype(BF16), w_f,
                                  fox_q_gain[j].reshape(1, hd), fox_k_gain[j].reshape(1, hd),
                                  B, S, H, hd, tq, tq // 2)
            caug = _cum_forget(f, fox_b_f[j], B, S, H)
            y = _fox_attn(qvt, k, caug, B, S, H, hd, tq)
            w_out = fox_w_out[j].astype(BF16)
        x2 = _out_proj(y, w_out, x2, gate1, S)
        x2 = _moe_layer(x2, g_ffn, shift2, scale2, gate2, moe_w_router[layer], moe_b_router[layer],
                        moe_w_gu, moe_b_gu, moe_w_down, moe_b_down, layer, S)
    return x2.reshape(B, S, D)
```

```python
import functools
import math

import jax
import jax.numpy as jnp
from jax import lax
from jax.experimental import pallas as pl
from jax.experimental.pallas import tpu as pltpu

F32 = jnp.float32
BF16 = jnp.bfloat16
I32 = jnp.int32
U32 = jnp.uint32

TOP_K = 4
N_MOD = 6
LRU_C = 8.0
SWIGLU_LIMIT = 7.0
SWIGLU_ALPHA = 1.702
EPS = 1e-6
LOG2E = 1.4426950408889634

AUG_ROWS = 16
LANES = 128
SUBLANES = 8
VMEM_LIMIT = 56 * 1024 * 1024
HIGHEST = lax.Precision.HIGHEST


def _params(sem, vmem=VMEM_LIMIT):
    return pltpu.CompilerParams(dimension_semantics=sem, vmem_limit_bytes=vmem)


def _tile(n, pref):
    t = min(n, pref)
    assert n % t == 0, (n, pref)
    return t


def _ada_kernel(ct_ref, w_ref, b_ref, o_ref, *, kc):
    K, nb = ct_ref.shape
    tn = w_ref.shape[1]

    def body(k, accs):
        r0 = pl.multiple_of(k * kc, kc)
        w = w_ref[pl.ds(r0, kc), :]
        c = ct_ref[pl.ds(r0, kc), :]
        s = c * jax.nn.sigmoid(c)
        return tuple(accs[b] + jnp.sum(w * s[:, b:b + 1], axis=0, keepdims=True) for b in range(nb))

    accs = lax.fori_loop(0, K // kc, body, tuple(jnp.zeros((1, tn), F32) for _ in range(nb)))
    for b in range(nb):
        o_ref[b:b + 1, :] = accs[b] + b_ref[...]


def _ada_proj(c, ada_w, ada_b):
    B, D = c.shape
    N = ada_w.shape[1]
    tn = _tile(N, 1024)
    kc = _tile(D, 256)
    return pl.pallas_call(
        functools.partial(_ada_kernel, kc=kc),
        grid=(N // tn,),
        in_specs=[pl.BlockSpec((D, B), lambda j: (0, 0)),
                  pl.BlockSpec((D, tn), lambda j: (0, j)),
                  pl.BlockSpec((1, tn), lambda j: (0, j))],
        out_specs=pl.BlockSpec((B, tn), lambda j: (0, j)),
        out_shape=jax.ShapeDtypeStruct((B, N), F32),
        compiler_params=_params(("parallel",)),
        name="ada_proj",
    )(c.T, ada_w, ada_b.reshape(1, N))


def _norm_mod_rows(x_ref, g_ref, sh_ref, sc_ref, emit, rc=64):
    tm = x_ref.shape[0]
    rc = min(rc, tm)
    gs = g_ref[...] * (1.0 + sc_ref[...])
    sh = sh_ref[...]

    def body(r, _):
        r0 = pl.multiple_of(r * rc, rc)
        x = x_ref[pl.ds(r0, rc), :]
        inv = lax.rsqrt(jnp.mean(x * x, axis=-1, keepdims=True) + EPS)
        emit(r0, rc, (x * inv) * gs + sh)
        return 0

    lax.fori_loop(0, tm // rc, body, 0)


def _norm_mm_kernel(x_ref, g_ref, sh_ref, sc_ref, w_ref, o_ref, h_ref):
    @pl.when(pl.program_id(1) == 0)
    def _():
        def emit(r0, rc, h):
            h_ref[pl.ds(r0, rc), :] = h.astype(BF16)
        _norm_mod_rows(x_ref, g_ref, sh_ref, sc_ref, emit)

    o_ref[...] = jnp.dot(h_ref[...], w_ref[...], preferred_element_type=F32).astype(o_ref.dtype)


def _norm_mm(x2, gain, shift, scale, w, S, tm=1024, tn=512):
    T, D = x2.shape
    N = w.shape[1]
    tm = _tile(S, tm)
    tn = _tile(N, tn)
    nt = S // tm
    return pl.pallas_call(
        _norm_mm_kernel,
        grid=(T // tm, N // tn),
        in_specs=[pl.BlockSpec((tm, D), lambda i, j: (i, 0)),
                  pl.BlockSpec((1, D), lambda i, j: (0, 0)),
                  pl.BlockSpec((None, 1, D), lambda i, j: (i // nt, 0, 0)),
                  pl.BlockSpec((None, 1, D), lambda i, j: (i // nt, 0, 0)),
                  pl.BlockSpec((D, tn), lambda i, j: (0, j))],
        out_specs=pl.BlockSpec((tm, tn), lambda i, j: (i, j)),
        out_shape=jax.ShapeDtypeStruct((T, N), BF16),
        scratch_shapes=[pltpu.VMEM((tm, D), BF16)],
        compiler_params=_params(("parallel", "arbitrary")),
        name="norm_mm",
    )(x2, gain, shift, scale, w)


def _gelu_tanh(x):
    c2 = 2.0 * math.sqrt(2.0 / math.pi)
    z2 = x * (c2 + (c2 * 0.044715) * (x * x))
    return x / (1.0 + jnp.exp(-z2))


def _lru_kernel(xb_ref, gb_ref, cw_ref, cb_ref, wax_ref, bax_ref, sp_ref, o_ref,
                xpad, a_s, b_s, hl_s, ac_s, carry):
    tt, C = xb_ref.shape
    L = tt // SUBLANES

    @pl.when(pl.program_id(2) == 0)
    def _():
        xpad[0:SUBLANES, :] = jnp.zeros((SUBLANES, C), F32)
        carry[...] = jnp.zeros_like(carry)

    xpad[SUBLANES:SUBLANES + tt, :] = xb_ref[...].astype(F32)
    cw = cw_ref[...]
    nk = cw.shape[0]
    xc = cb_ref[...] + cw[nk - 1:nk, :] * xpad[SUBLANES:SUBLANES + tt, :]
    for d in range(1, nk):
        xc = xc + cw[nk - 1 - d:nk - d, :] * xpad[SUBLANES - d:SUBLANES - d + tt, :]
    xpad[0:SUBLANES, :] = xpad[tt:tt + SUBLANES, :]

    ri = jnp.dot(xc.astype(BF16), wax_ref[...], preferred_element_type=F32) + bax_ref[...]
    r = jax.nn.sigmoid(ri[:, :C])
    ig = jax.nn.sigmoid(ri[:, C:])
    log_a = (-LRU_C) * r * sp_ref[...]
    a = jnp.exp(log_a)
    om = 1.0 - a * a
    mult = jnp.where(om > 0.0, om * lax.rsqrt(om), 0.0)
    nlv = C // LANES
    bt = mult * (ig * xc)
    for c in range(nlv):
        a_s[c] = a[:, c * LANES:(c + 1) * LANES]
        b_s[c] = bt[:, c * LANES:(c + 1) * LANES]

    def body(j, hA):
        j8 = pl.multiple_of(j * SUBLANES, SUBLANES)
        out = []
        for c in range(nlv):
            h, A = hA[c]
            aj = a_s[c, pl.ds(j, SUBLANES, stride=L), :]
            bj = b_s[c, pl.ds(j, SUBLANES, stride=L), :]
            h = aj * h + bj
            A = aj * A
            hl_s[c, pl.ds(j8, SUBLANES), :] = h
            ac_s[c, pl.ds(j8, SUBLANES), :] = A
            out.append((h, A))
        return tuple(out)

    init = tuple((jnp.zeros((SUBLANES, LANES), F32), jnp.ones((SUBLANES, LANES), F32)) for _ in range(nlv))
    ends = lax.fori_loop(0, L, body, init, unroll=min(L, 8))

    for c in range(nlv):
        h_end, a_end = ends[c]
        cs = slice(c * LANES, (c + 1) * LANES)
        e = carry[0:1, cs]
        for s in range(SUBLANES):
            hl = hl_s[c, pl.ds(s, L, stride=SUBLANES), :]
            ac = ac_s[c, pl.ds(s, L, stride=SUBLANES), :]
            hs = hl + ac * e
            gate = _gelu_tanh(gb_ref[s * L:(s + 1) * L, cs].astype(F32))
            o_ref[s * L:(s + 1) * L, cs] = (hs * gate).astype(o_ref.dtype)
            e = h_end[s:s + 1, :] + a_end[s:s + 1, :] * e
        carry[0:1, cs] = e


def _lru_core(u, conv_w, conv_b, w_ax, b_ax, softplus_neg_lam, B, S, tt=2048):
    T, C2 = u.shape
    C = C2 // 2
    nh, hb, _ = w_ax.shape
    tt = _tile(S, tt)
    nt = S // tt
    return pl.pallas_call(
        _lru_kernel,
        grid=(B, nh, nt),
        in_specs=[pl.BlockSpec((tt, hb), lambda b, h, t: (b * nt + t, h)),
                  pl.BlockSpec((tt, hb), lambda b, h, t: (b * nt + t, nh + h)),
                  pl.BlockSpec((conv_w.shape[0], hb), lambda b, h, t: (0, h)),
                  pl.BlockSpec((1, hb), lambda b, h, t: (0, h)),
                  pl.BlockSpec((None, hb, 2 * hb), lambda b, h, t: (h, 0, 0)),
                  pl.BlockSpec((None, 1, 2 * hb), lambda b, h, t: (h, 0, 0)),
                  pl.BlockSpec((1, hb), lambda b, h, t: (0, h))],
        out_specs=pl.BlockSpec((tt, hb), lambda b, h, t: (b * nt + t, h)),
        out_shape=jax.ShapeDtypeStruct((T, C), BF16),
        scratch_shapes=[pltpu.VMEM((tt + SUBLANES, hb), F32),
                        pltpu.VMEM((hb // LANES, tt, LANES), F32), pltpu.VMEM((hb // LANES, tt, LANES), F32),
                        pltpu.VMEM((hb // LANES, tt, LANES), F32), pltpu.VMEM((hb // LANES, tt, LANES), F32),
                        pltpu.VMEM((SUBLANES, hb), F32)],
        compiler_params=_params(("parallel", "parallel", "arbitrary")),
        name="lru_core",
    )(u, u, conv_w, conv_b, w_ax, b_ax, softplus_neg_lam)


def _out_proj_kernel(y_ref, w_ref, x_ref, g_ref, o_ref):
    o_ref[...] = x_ref[...] + g_ref[...] * jnp.dot(y_ref[...], w_ref[...], preferred_element_type=F32)


def _out_proj(y, w, x2, gate, S, tm=1024, tn=1024):
    T, D = x2.shape
    K = y.shape[1]
    tm = _tile(S, tm)
    tn = _tile(D, tn)
    nt = S // tm
    return pl.pallas_call(
        _out_proj_kernel,
        grid=(T // tm, D // tn),
        in_specs=[pl.BlockSpec((tm, K), lambda i, j: (i, 0)),
                  pl.BlockSpec((K, tn), lambda i, j: (0, j)),
                  pl.BlockSpec((tm, tn), lambda i, j: (i, j)),
                  pl.BlockSpec((None, 1, tn), lambda i, j: (i // nt, 0, j))],
        out_specs=pl.BlockSpec((tm, tn), lambda i, j: (i, j)),
        out_shape=jax.ShapeDtypeStruct((T, D), F32),
        compiler_params=_params(("parallel", "parallel")),
        name="out_proj",
    )(y, w, x2, gate)


def _pack_bf16_pair(lo, hi):
    ul = lax.bitcast_convert_type(lo, U32)
    uh = lax.bitcast_convert_type(hi, U32)
    rl = (ul + 0x8000) >> 16
    rh = ((uh + 0x8000) >> 16) << 16
    return rh | rl


def _unpack_bf16_pair(w):
    lo = lax.bitcast_convert_type(w << 16, F32)
    hi = lax.bitcast_convert_type((w >> 16) << 16, F32)
    return lo, hi


def _unpack_rows_bf16(w):
    lo, hi = _unpack_bf16_pair(w)
    return jnp.concatenate([lo.astype(BF16), hi.astype(BF16)], axis=1)


def _router_kernel(x_ref, g_ref, sh_ref, sc_ref, wr_ref, br_ref,
                   hp_ref, idx_ref, gate_ref, rank_ref, cnt_ref, carry, lg_ref, *, n_exp):
    tm, D = x_ref.shape

    @pl.when(pl.program_id(0) == 0)
    def _():
        carry[...] = jnp.zeros_like(carry)

    def emit(r0, rc, h):
        hp_ref[pl.ds(r0, rc), :] = _pack_bf16_pair(h[:, :D // 2], h[:, D // 2:])
        lg_ref[pl.ds(r0, rc), :] = jnp.dot(h.astype(BF16), wr_ref[...], preferred_element_type=F32)
    _norm_mod_rows(x_ref, g_ref, sh_ref, sc_ref, emit)

    logits = lg_ref[...] + br_ref[...]
    v = logits.T[:n_exp, :]
    eio = lax.broadcasted_iota(I32, (n_exp, tm), 0)
    vals, idxs = [], []
    for _ in range(TOP_K):
        m = jnp.max(v, axis=0, keepdims=True)
        ik = jnp.min(jnp.where(v == m, eio, n_exp), axis=0, keepdims=True)
        vals.append(m)
        idxs.append(ik)
        v = jnp.where(eio == ik, -jnp.inf, v)
    ex = [jnp.exp(vk - vals[0]) for vk in vals]
    den = ex[0] + ex[1] + ex[2] + ex[3]

    onehot = jnp.zeros((n_exp, tm), F32)
    for ik in idxs:
        onehot = onehot + (eio == ik).astype(F32)
    si = lax.broadcasted_iota(I32, (tm, tm), 0)
    ti = lax.broadcasted_iota(I32, (tm, tm), 1)
    before = (si < ti).astype(BF16)
    base = jnp.dot(onehot.astype(BF16), before, preferred_element_type=F32) + carry[:, 0:1]

    zi = jnp.zeros((SUBLANES - TOP_K, tm), I32)
    idx_ref[...] = jnp.concatenate(idxs + [zi], axis=0)
    gate_ref[...] = jnp.concatenate([e / den for e in ex] + [zi.astype(F32)], axis=0)
    ranks = [jnp.sum(jnp.where(eio == ik, base, 0.0), axis=0, keepdims=True).astype(I32) for ik in idxs]
    rank_ref[...] = jnp.concatenate(ranks + [zi], axis=0)

    carry[...] = carry[...] + jnp.sum(onehot, axis=1, keepdims=True)
    cnt_ref[...] = carry[...]


def _router(x2, gain, shift, scale, w_r, b_r, S, tm=512):
    T, D = x2.shape
    E = w_r.shape[1]
    tm = _tile(S, tm)
    nt = S // tm
    wr = jnp.zeros((D, LANES), BF16).at[:, :E].set(w_r.astype(BF16))
    br = jnp.zeros((1, LANES), F32).at[0, :E].set(b_r)
    return pl.pallas_call(
        functools.partial(_router_kernel, n_exp=E),
        grid=(T // tm,),
        in_specs=[pl.BlockSpec((tm, D), lambda i: (i, 0)),
                  pl.BlockSpec((1, D), lambda i: (0, 0)),
                  pl.BlockSpec((None, 1, D), lambda i: (i // nt, 0, 0)),
                  pl.BlockSpec((None, 1, D), lambda i: (i // nt, 0, 0)),
                  pl.BlockSpec((D, LANES), lambda i: (0, 0)),
                  pl.BlockSpec((1, LANES), lambda i: (0, 0))],
        out_specs=[pl.BlockSpec((tm, D // 2), lambda i: (i, 0)),
                   pl.BlockSpec((SUBLANES, tm), lambda i: (0, i)),
                   pl.BlockSpec((SUBLANES, tm), lambda i: (0, i)),
                   pl.BlockSpec((SUBLANES, tm), lambda i: (0, i)),
                   pl.BlockSpec((E, LANES), lambda i: (0, 0))],
        out_shape=[jax.ShapeDtypeStruct((T, D // 2), U32),
                   jax.ShapeDtypeStruct((SUBLANES, T), I32),
                   jax.ShapeDtypeStruct((SUBLANES, T), F32),
                   jax.ShapeDtypeStruct((SUBLANES, T), I32),
                   jax.ShapeDtypeStruct((E, LANES), F32)],
        scratch_shapes=[pltpu.VMEM((E, LANES), F32), pltpu.VMEM((tm, LANES), F32)],
        compiler_params=_params(("arbitrary",)),
        name="moe_router",
    )(x2, gain, shift, scale, wr, br)


def _dispatch_kernel(dest_ref, cnt_ref, pst_ref, pad_ref, nu_ref, h_ref, xs_ref, zblk, sem):
    tc = h_ref.shape[0]
    tm_e = zblk.shape[0]
    i = pl.program_id(0)
    n_tok = pl.num_programs(0) * tc
    n_exp = cnt_ref.shape[0]
    n_blk = xs_ref.shape[0] // tm_e

    def copy(t, k):
        d = dest_ref[k * n_tok + i * tc + t]
        return pltpu.make_async_copy(h_ref.at[pl.ds(t, 1)], xs_ref.at[pl.ds(d, 1)], sem.at[0])

    def pad_copy(r):
        return pltpu.make_async_copy(zblk.at[pl.ds(0, 1)], xs_ref.at[pl.ds(r, 1)], sem.at[1])

    def blk_copy(b):
        r0 = pl.multiple_of(b * tm_e, tm_e)
        return pltpu.make_async_copy(zblk, xs_ref.at[pl.ds(r0, tm_e)], sem.at[2])

    def for_pad_rows(fn):
        def per_expert(e, _):
            def per_row(r, _):
                fn(pad_copy(r))
                return 0
            lax.fori_loop(pst_ref[e] + cnt_ref[e], pst_ref[e] + pad_ref[e], per_row, 0)
            return 0
        lax.fori_loop(0, n_exp, per_expert, 0)

        def per_blk(b, _):
            fn(blk_copy(b))
            return 0
        lax.fori_loop(nu_ref[0], n_blk, per_blk, 0)

    @pl.when(i == 0)
    def _():
        zblk[...] = jnp.zeros(zblk.shape, zblk.dtype)
        for_pad_rows(lambda c: c.start())

    def start(t, _):
        for k in range(TOP_K):
            copy(t, k).start()
        return 0

    def wait(t, _):
        for k in range(TOP_K):
            copy(t, k).wait()
        return 0

    lax.fori_loop(0, tc, start, 0)
    lax.fori_loop(0, tc, wait, 0)

    @pl.when(i == 0)
    def _():
        for_pad_rows(lambda c: c.wait())


def _dispatch(dest_flat, counts, pstart, padded, n_used, h, n_rows, tm_e, tc=256):
    T, D = h.shape
    tc = _tile(T, tc)
    return pl.pallas_call(
        _dispatch_kernel,
        grid_spec=pltpu.PrefetchScalarGridSpec(
            num_scalar_prefetch=5,
            grid=(T // tc,),
            in_specs=[pl.BlockSpec((tc, D), lambda i, *_: (i, 0))],
            out_specs=pl.BlockSpec(memory_space=pl.ANY),
            scratch_shapes=[pltpu.VMEM((tm_e, D), h.dtype), pltpu.SemaphoreType.DMA((3,))]),
        out_shape=jax.ShapeDtypeStruct((n_rows, D), h.dtype),
        compiler_params=_params(("arbitrary",)),
        name="moe_dispatch",
    )(dest_flat, counts, pstart, padded, n_used, h)


def _refresh_expert_weight(be_ref, w_ref, wbf_ref, rows=256):
    i = pl.program_id(0)
    changed = (i == 0) | (be_ref[i] != be_ref[jnp.maximum(i - 1, 0)])

    @pl.when(changed)
    def _():
        rc = min(rows, w_ref.shape[0])

        def body(r, _):
            r0 = pl.multiple_of(r * rc, rc)
            wbf_ref[pl.ds(r0, rc), :] = w_ref[pl.ds(r0, rc), :].astype(BF16)
            return 0

        lax.fori_loop(0, w_ref.shape[0] // rc, body, 0)


def _expert_gate_kernel(be_ref, nu_ref, xs_ref, w_ref, b_ref, o_ref, wbf_ref):
    _refresh_expert_weight(be_ref, w_ref, wbf_ref)
    i = pl.program_id(0)

    @pl.when(i < nu_ref[0])
    def _():
        g = jnp.dot(_unpack_rows_bf16(xs_ref[...]), wbf_ref[...], preferred_element_type=F32) + b_ref[...]
        g = jnp.minimum(g, SWIGLU_LIMIT)
        o_ref[...] = (g * jax.nn.sigmoid(SWIGLU_ALPHA * g)).astype(o_ref.dtype)

    @pl.when(i >= nu_ref[0])
    def _():
        o_ref[...] = jnp.zeros_like(o_ref)


def _expert_up_kernel(be_ref, nu_ref, xs_ref, w_ref, b_ref, glu_ref, o_ref, wbf_ref):
    _refresh_expert_weight(be_ref, w_ref, wbf_ref)
    i = pl.program_id(0)

    @pl.when(i < nu_ref[0])
    def _():
        up = jnp.dot(_unpack_rows_bf16(xs_ref[...]), wbf_ref[...], preferred_element_type=F32) + b_ref[...]
        up = jnp.clip(up, -SWIGLU_LIMIT, SWIGLU_LIMIT)
        o_ref[...] = ((up + 1.0) * glu_ref[...].astype(F32)).astype(o_ref.dtype)

    @pl.when(i >= nu_ref[0])
    def _():
        o_ref[...] = jnp.zeros_like(o_ref)


def _expert_down_kernel(be_ref, nu_ref, a_ref, w_ref, b_ref, o_ref, wbf_ref):
    _refresh_expert_weight(be_ref, w_ref, wbf_ref)
    i = pl.program_id(0)

    @pl.when(i < nu_ref[0])
    def _():
        y = jnp.dot(a_ref[...], wbf_ref[...], preferred_element_type=F32) + b_ref[...]
        dh = y.shape[1] // 2
        o_ref[...] = _pack_bf16_pair(y[:, :dh], y[:, dh:])

    @pl.when(i >= nu_ref[0])
    def _():
        o_ref[...] = jnp.zeros_like(o_ref)


def _experts(blk_expert, n_used, xs, w_gu, b_gu, w_down, b_down, layer, tm):
    n_rows, Dh = xs.shape
    _, E, D, F2 = w_gu.shape
    F = F2 // 2
    n_blk = n_rows // tm
    b_gu3 = b_gu[layer].reshape(E, 1, F2)

    def row_idx(i, be, nu):
        return (jnp.minimum(i, nu[0] - 1), 0)

    def call(kern, name, half, extra_in, extra_specs):
        return pl.pallas_call(
            kern,
            grid_spec=pltpu.PrefetchScalarGridSpec(
                num_scalar_prefetch=2,
                grid=(n_blk,),
                in_specs=[pl.BlockSpec((tm, Dh), row_idx),
                          pl.BlockSpec((None, None, D, F), lambda i, be, nu: (layer, be[i], 0, half)),
                          pl.BlockSpec((None, 1, F), lambda i, be, nu: (be[i], 0, half))] + extra_specs,
                out_specs=pl.BlockSpec((tm, F), lambda i, be, nu: (i, 0)),
                scratch_shapes=[pltpu.VMEM((D, F), BF16)]),
            out_shape=jax.ShapeDtypeStruct((n_rows, F), BF16),
            compiler_params=_params(("arbitrary",)),
            name=name,
        )(blk_expert, n_used, xs, w_gu, b_gu3, *extra_in)

    glu = call(_expert_gate_kernel, "moe_gate", 0, [], [])
    act = call(_expert_up_kernel, "moe_up", 1, [glu], [pl.BlockSpec((tm, F), row_idx)])

    return pl.pallas_call(
        _expert_down_kernel,
        grid_spec=pltpu.PrefetchScalarGridSpec(
            num_scalar_prefetch=2,
            grid=(n_blk,),
            in_specs=[pl.BlockSpec((tm, F), row_idx),
                      pl.BlockSpec((None, None, F, D), lambda i, be, nu: (layer, be[i], 0, 0)),
                      pl.BlockSpec((None, 1, D), lambda i, be, nu: (be[i], 0, 0))],
            out_specs=pl.BlockSpec((tm, Dh), lambda i, be, nu: (i, 0)),
            scratch_shapes=[pltpu.VMEM((F, D), BF16)]),
        out_shape=jax.ShapeDtypeStruct((n_rows, Dh), U32),
        compiler_params=_params(("arbitrary",)),
        name="moe_down",
    )(blk_expert, n_used, act, w_down, b_down[layer].reshape(E, 1, D))


def _combine_kernel(dest_ref, ys_ref, x_ref, g5_ref, gt_ref, o_ref, buf, gcol, sem):
    tc, D = x_ref.shape
    dh = D // 2
    lc = dh // SUBLANES
    i = pl.program_id(0)
    n = pl.num_programs(0)
    n_tok = n * tc

    def copy(tile, slot, t, k):
        d = dest_ref[k * n_tok + tile * tc + t]
        return pltpu.make_async_copy(ys_ref.at[pl.ds(d, 1)], buf.at[slot, k, pl.ds(t, 1)], sem.at[slot])

    def start_tile(tile, slot):
        def body(t, _):
            for k in range(TOP_K):
                copy(tile, slot, t, k).start()
            return 0
        lax.fori_loop(0, tc, body, 0)

    def wait_tile(tile, slot):
        def body(t, _):
            for k in range(TOP_K):
                copy(tile, slot, t, k).wait()
            return 0
        lax.fori_loop(0, tc, body, 0)

    slot = i % 2

    @pl.when(i == 0)
    def _():
        start_tile(0, 0)

    wait_tile(i, slot)
    diag = lax.broadcasted_iota(I32, (tc, tc), 0) == lax.broadcasted_iota(I32, (tc, tc), 1)
    for k in range(TOP_K):
        col = jnp.sum(jnp.where(diag, gt_ref[k:k + 1, :], 0.0), axis=1, keepdims=True)
        gcol[k] = jnp.broadcast_to(col, (tc, LANES))

    def group(g, issue_next):
        r0 = pl.multiple_of(g * SUBLANES, SUBLANES)
        rows = pl.ds(r0, SUBLANES)
        cb = []
        for k in range(TOP_K):
            v = gcol[k, rows, :]
            cb.append(v[:, :lc] if lc <= LANES else jnp.concatenate([v] * (lc // LANES), axis=1))
        for t in range(SUBLANES):
            if issue_next:
                for k in range(TOP_K):
                    copy(i + 1, 1 - slot, r0 + t, k).start()
            c0 = t * lc
            acc_lo = acc_hi = None
            for k in range(TOP_K):
                lo, hi = _unpack_bf16_pair(buf[slot, k, rows, c0:c0 + lc])
                acc_lo = cb[k] * lo if acc_lo is None else acc_lo + cb[k] * lo
                acc_hi = cb[k] * hi if acc_hi is None else acc_hi + cb[k] * hi
            o_ref[rows, c0:c0 + lc] = x_ref[rows, c0:c0 + lc] + g5_ref[:, c0:c0 + lc] * acc_lo
            o_ref[rows, dh + c0:dh + c0 + lc] = (x_ref[rows, dh + c0:dh + c0 + lc]
                                                 + g5_ref[:, dh + c0:dh + c0 + lc] * acc_hi)
        return 0

    @pl.when(i + 1 < n)
    def _():
        lax.fori_loop(0, tc // SUBLANES, lambda g, _: group(g, True), 0)

    @pl.when(i + 1 >= n)
    def _():
        lax.fori_loop(0, tc // SUBLANES, lambda g, _: group(g, False), 0)


def _combine(dest_flat, ys, x2, gate5, gates, S, tc=128):
    T, D = x2.shape
    tc = _tile(S, tc)
    nt = S // tc
    return pl.pallas_call(
        _combine_kernel,
        grid_spec=pltpu.PrefetchScalarGridSpec(
            num_scalar_prefetch=1,
            grid=(T // tc,),
            in_specs=[pl.BlockSpec(memory_space=pl.ANY),
                      pl.BlockSpec((tc, D), lambda i, d: (i, 0)),
                      pl.BlockSpec((None, 1, D), lambda i, d: (i // nt, 0, 0)),
                      pl.BlockSpec((SUBLANES, tc), lambda i, d: (0, i))],
            out_specs=pl.BlockSpec((tc, D), lambda i, d: (i, 0)),
            scratch_shapes=[pltpu.VMEM((2, TOP_K, tc, D // 2), ys.dtype),
                            pltpu.VMEM((TOP_K, tc, LANES), F32),
                            pltpu.SemaphoreType.DMA((2,))]),
        out_shape=jax.ShapeDtypeStruct((T, D), F32),
        compiler_params=_params(("arbitrary",)),
        name="moe_combine",
    )(dest_flat, ys, x2, gate5, gates)


def _moe_layer(x2, gain, shift, scale, gate5, w_r, b_r, w_gu, b_gu, w_down, b_down, layer, S, tm_e=512):
    T, D = x2.shape
    E = w_r.shape[1]
    h, idx, gates, rank, cnt = _router(x2, gain, shift, scale, w_r, b_r, S)
    counts = cnt[:, 0].astype(I32)
    padded = (counts + tm_e - 1) // tm_e * tm_e
    pend = jnp.cumsum(padded)
    pstart = pend - padded
    onehot = idx[:TOP_K, :, None] == jnp.arange(E, dtype=I32)
    dest = (jnp.sum(jnp.where(onehot, pstart, 0), axis=-1) + rank[:TOP_K]).reshape(TOP_K * T)
    n_rows = (T * TOP_K // tm_e + E) * tm_e
    n_blk = n_rows // tm_e
    n_used = (pend[E - 1] // tm_e).astype(I32)
    bi = jnp.minimum(jnp.arange(n_blk, dtype=I32), n_used - 1)
    blk_expert = jnp.minimum(jnp.sum(pend[None, :] <= (bi * tm_e)[:, None], axis=1), E - 1).astype(I32)

    n_used = n_used.reshape(1)
    xs = _dispatch(dest, counts, pstart, padded, n_used, h, n_rows, tm_e)
    ys = _experts(blk_expert, n_used, xs, w_gu, b_gu, w_down, b_down, layer, tm_e)
    return _combine(dest, ys, x2, gate5, gates, S)


def _fox_qkv_kernel(x_ref, g_ref, sh_ref, sc_ref, w_ref, wf_ref, gq_ref, gk_ref, ot_ref, ok_ref, f_ref, h_ref,
                    *, n_q_tiles, hd, q_scale):
    j = pl.program_id(1)
    hpt, n_sub, _, tk = ot_ref.shape

    @pl.when(j == 0)
    def _():
        def emit(r0, rc, h):
            h_ref[pl.ds(r0, rc), :] = h.astype(BF16)
        _norm_mod_rows(x_ref, g_ref, sh_ref, sc_ref, emit)
        f_ref[...] = jnp.dot(h_ref[...], wf_ref[...], preferred_element_type=F32)

    def head_norm(yh, gain):
        inv = lax.rsqrt(jnp.mean(yh * yh, axis=-1, keepdims=True) + EPS)
        return (yh * inv) * gain

    def chunks(epilogue):
        for s in range(n_sub):
            y = jnp.dot(h_ref[s * tk:(s + 1) * tk, :], w_ref[...], preferred_element_type=F32)
            for hh in range(hpt):
                epilogue(hh, s, y[:, hh * hd:(hh + 1) * hd])

    aug_row = lax.broadcasted_iota(I32, (AUG_ROWS, tk), 0)
    q_aug = jnp.where(aug_row < 3, -1.0, 0.0).astype(ot_ref.dtype)
    v_aug = jnp.where(aug_row < 1, 1.0, 0.0).astype(ot_ref.dtype)

    @pl.when(j < n_q_tiles)
    def _():
        gain = gq_ref[...] * q_scale

        def epilogue(hh, s, yh):
            ot_ref[hh, s, 0:hd, :] = head_norm(yh, gain).T.astype(ot_ref.dtype)
            ot_ref[hh, s, hd:hd + AUG_ROWS, :] = q_aug
        chunks(epilogue)

    @pl.when((j >= n_q_tiles) & (j < 2 * n_q_tiles))
    def _():
        def epilogue(hh, s, yh):
            ok_ref[hh, s * tk:(s + 1) * tk, :] = head_norm(yh, gk_ref[...]).astype(ok_ref.dtype)
        chunks(epilogue)

    @pl.when(j >= 2 * n_q_tiles)
    def _():
        def epilogue(hh, s, yh):
            ot_ref[hh, s, 0:hd, :] = yh.T.astype(ot_ref.dtype)
            ot_ref[hh, s, hd:hd + AUG_ROWS, :] = v_aug
        chunks(epilogue)


def _fox_proj(x2, gain, shift, scale, w_in, w_f, q_gain, k_gain, B, S, H, hd, tm, tk, hpt=8):
    T, D = x2.shape
    hpt = _tile(H, hpt)
    tn = hpt * hd
    nt = S // tm
    n_sub = tm // tk
    n_q_tiles = H // hpt
    q_scale = (hd ** -0.5) * LOG2E

    nqt = n_q_tiles

    def t_idx(j):
        return jnp.where(j < nqt, j, jnp.where(j < 2 * nqt, nqt - 1, j - nqt))

    def k_idx(j):
        return jnp.clip(j - nqt, 0, nqt - 1)

    qvt, k, f = pl.pallas_call(
        functools.partial(_fox_qkv_kernel, n_q_tiles=nqt, hd=hd, q_scale=q_scale),
        grid=(T // tm, 3 * nqt),
        in_specs=[pl.BlockSpec((tm, D), lambda i, j: (i, 0)),
                  pl.BlockSpec((1, D), lambda i, j: (0, 0)),
                  pl.BlockSpec((None, 1, D), lambda i, j: (i // nt, 0, 0)),
                  pl.BlockSpec((None, 1, D), lambda i, j: (i // nt, 0, 0)),
                  pl.BlockSpec((D, tn), lambda i, j: (0, j)),
                  pl.BlockSpec((D, LANES), lambda i, j: (0, 0)),
                  pl.BlockSpec((1, hd), lambda i, j: (0, 0)),
                  pl.BlockSpec((1, hd), lambda i, j: (0, 0))],
        out_specs=[pl.BlockSpec((None, hpt, n_sub, hd + AUG_ROWS, tk),
                                lambda i, j: (i // nt, t_idx(j), i % nt, 0, 0)),
                   pl.BlockSpec((None, hpt, tm, hd), lambda i, j: (i // nt, k_idx(j), i % nt, 0)),
                   pl.BlockSpec((tm, LANES), lambda i, j: (i, 0))],
        out_shape=[jax.ShapeDtypeStruct((B, 2 * H, S // tk, hd + AUG_ROWS, tk), BF16),
                   jax.ShapeDtypeStruct((B, H, S, hd), BF16),
                   jax.ShapeDtypeStruct((T, LANES), F32)],
        scratch_shapes=[pltpu.VMEM((tm, D), BF16)],
        compiler_params=_params(("parallel", "arbitrary")),
        name="fox_qkv_proj",
    )(x2, gain, shift, scale, w_in, w_f, q_gain, k_gain)
    return qvt, k, f


def _cum_kernel(f_ref, bf_ref, o_ref, carry):
    tt = f_ref.shape[0]

    @pl.when(pl.program_id(1) == 0)
    def _():
        carry[...] = jnp.zeros_like(carry)

    logf = jax.nn.log_sigmoid(f_ref[...] + bf_ref[...])
    si = lax.broadcasted_iota(I32, (tt, tt), 0)
    ti = lax.broadcasted_iota(I32, (tt, tt), 1)
    incl = (ti <= si).astype(F32)
    c = jnp.dot(incl, logf, preferred_element_type=F32, precision=HIGHEST) + carry[0:1, :]
    carry[0:1, :] = c[tt - 1:tt, :]
    c2 = c * LOG2E
    hi = c2.astype(BF16).astype(F32)
    r1 = c2 - hi
    mid = r1.astype(BF16).astype(F32)
    lo = (r1 - mid).astype(BF16).astype(F32)
    lane = lax.broadcasted_iota(I32, (tt, LANES), 1)
    for h in range(o_ref.shape[0]):
        t = jnp.where(lane == 0, hi[:, h:h + 1],
                      jnp.where(lane == 1, mid[:, h:h + 1],
                                jnp.where(lane == 2, lo[:, h:h + 1], 0.0)))
        o_ref[h] = t.astype(o_ref.dtype)


def _cum_forget(f, b_f, B, S, H, tt=256):
    T = f.shape[0]
    tt = _tile(S, tt)
    nt = S // tt
    bf = jnp.zeros((1, LANES), F32).at[0, :H].set(b_f)
    return pl.pallas_call(
        _cum_kernel,
        grid=(B, nt),
        in_specs=[pl.BlockSpec((tt, LANES), lambda b, t: (b * nt + t, 0)),
                  pl.BlockSpec((1, LANES), lambda b, t: (0, 0))],
        out_specs=pl.BlockSpec((None, H, tt, LANES), lambda b, t: (b, 0, t, 0)),
        out_shape=jax.ShapeDtypeStruct((B, H, S, LANES), BF16),
        scratch_shapes=[pltpu.VMEM((SUBLANES, LANES), F32)],
        compiler_params=_params(("parallel", "arbitrary")),
        name="fox_cum",
    )(f, bf)


def _fox_attn_kernel(qt_ref, k_ref, c_ref, vt_ref, o_ref, *scratch):
    hd = k_ref.shape[2]
    for g in range(qt_ref.shape[0]):
        _fox_attn_head(qt_ref.at[g], k_ref.at[g], c_ref.at[g], vt_ref.at[g],
                       o_ref.at[:, g * hd:(g + 1) * hd], *scratch)


def _fox_attn_head(qt_ref, k_ref, c_ref, vt_ref, o_ref, s_a, s_b, p_a, p_b, m_ref, acc_ref):
    n_sub, hda, tk = qt_ref.shape
    hd = hda - AUG_ROWS
    tq = n_sub * tk
    assert n_sub == 2
    qi = pl.program_id(2)
    qt = jnp.concatenate([qt_ref[s] for s in range(n_sub)], axis=1)
    qp = jnp.concatenate([qt, jnp.zeros((hd - AUG_ROWS, tq), BF16)], axis=0)

    def scores(j):
        r0 = pl.multiple_of(j * tk, tk)
        kp = jnp.concatenate([k_ref[pl.ds(r0, tk), :], c_ref[pl.ds(r0, tk), :]], axis=1)
        return jnp.dot(kp, qp, preferred_element_type=F32)

    def stage(j, s_cur, s_nxt, p_prev, p_cur, mask=None):
        if s_nxt is not None:
            s_nxt[...] = scores(j + 1)
        pv = jnp.dot(vt_ref[jnp.maximum(j - 1, 0)], p_prev[...], preferred_element_type=F32)
        t = s_cur[...]
        if mask is not None:
            t = jnp.where(mask, t, -jnp.inf)
        m_old = m_ref[...]
        m_new = jnp.maximum(m_old, jnp.max(t, axis=0, keepdims=True))
        alpha = jnp.exp2(m_old - m_new)
        p_cur[...] = jnp.exp2(t - m_new).astype(BF16)
        acc_ref[...] = alpha * (acc_ref[...] + pv)
        m_ref[...] = m_new

    m_ref[...] = jnp.full(m_ref.shape, -jnp.inf, F32)
    acc_ref[...] = jnp.zeros(acc_ref.shape, F32)
    p_b[...] = jnp.zeros(p_b.shape, BF16)
    s_a[...] = scores(0)

    def pair(p):
        stage(2 * p, s_a, s_b, p_b, p_a)
        stage(2 * p + 1, s_b, s_a, p_a, p_b)

    def trip(i, _):
        pair(3 * i)
        pair(3 * i + 1)
        pair(3 * i + 2)
        return 0

    lax.fori_loop(0, qi // 3, trip, 0)
    rem = qi % 3

    @pl.when(rem >= 1)
    def _():
        pair(qi - rem)

    @pl.when(rem == 2)
    def _():
        pair(qi - 1)

    key = lax.broadcasted_iota(I32, (tk, tq), 0)
    qry = lax.broadcasted_iota(I32, (tk, tq), 1)
    stage(2 * qi, s_a, s_b, p_b, p_a, mask=key <= qry)
    stage(2 * qi + 1, s_b, None, p_a, p_b, mask=key + tk <= qry)
    acc = acc_ref[...] + jnp.dot(vt_ref[2 * qi + 1], p_b[...], preferred_element_type=F32)
    o_ref[...] = (acc[:hd] / acc[hd:hd + 1]).T.astype(o_ref.dtype)


def _fox_attn(qvt, k, caug, B, S, H, hd, tq, G=2):
    nq = S // tq
    tk = qvt.shape[-1]
    nk = S // tk
    n_sub = tq // tk
    hda = hd + AUG_ROWS
    ng = H // G
    return pl.pallas_call(
        _fox_attn_kernel,
        grid=(B, ng, nq),
        in_specs=[pl.BlockSpec((None, G, n_sub, hda, tk), lambda b, h, q: (b, h, q, 0, 0)),
                  pl.BlockSpec((None, G, S, hd), lambda b, h, q: (b, h, 0, 0)),
                  pl.BlockSpec((None, G, S, LANES), lambda b, h, q: (b, h, 0, 0)),
                  pl.BlockSpec((None, G, nk, hda, tk), lambda b, h, q: (b, ng + h, 0, 0, 0))],
        out_specs=pl.BlockSpec((tq, G * hd), lambda b, h, q: (b * nq + q, h)),
        out_shape=jax.ShapeDtypeStruct((B * S, H * hd), BF16),
        scratch_shapes=[pltpu.VMEM((tk, tq), F32), pltpu.VMEM((tk, tq), F32),
                        pltpu.VMEM((tk, tq), BF16), pltpu.VMEM((tk, tq), BF16),
                        pltpu.VMEM((1, tq), F32), pltpu.VMEM((hda, tq), F32)],
        compiler_params=_params(("parallel", "parallel", "arbitrary")),
        name="fox_attn",
    )(qvt, k, caug, qvt)


def kernel(x, c, ada_w, ada_b, ada_table, norm_mix_g, norm_ffn_g, lru_w_in, lru_conv_w, lru_conv_b, lru_w_a, lru_b_a, lru_w_x, lru_b_x, lru_lam, lru_w_out, fox_w_in, fox_b_f, fox_q_gain, fox_k_gain, fox_w_out, moe_w_router, moe_b_router, moe_w_gu, moe_b_gu, moe_w_down, moe_b_down):
    B, S, D = x.shape
    T = B * S
    depth = ada_table.shape[0]
    H = fox_b_f.shape[1]
    hd = D // H
    assert hd == LANES and ada_table.shape[1] == N_MOD

    m = _ada_proj(c, ada_w, ada_b)
    mod = (m[None] + ada_table.reshape(depth, 1, N_MOD * D)).reshape(depth, B, N_MOD, 1, D)
    x2 = x.reshape(T, D)

    for layer in range(depth):
        shift1, scale1, gate1, shift2, scale2, gate2 = (mod[layer, :, i] for i in range(N_MOD))
        g_mix = norm_mix_g[layer].reshape(1, D)
        g_ffn = norm_ffn_g[layer].reshape(1, D)
        j = layer // 2
        if layer % 2 == 0:
            nh, hb, _ = lru_w_a[j].shape
            assert hb % LANES == 0
            u = _norm_mm(x2, g_mix, shift1, scale1, lru_w_in[j].astype(BF16), S)
            w_ax = jnp.concatenate([lru_w_a[j], lru_w_x[j]], axis=-1).astype(BF16)
            b_ax = jnp.concatenate([lru_b_a[j].reshape(nh, 1, hb), lru_b_x[j].reshape(nh, 1, hb)], axis=-1)
            sp = jax.nn.softplus(-lru_lam[j]).reshape(1, nh * hb)
            y = _lru_core(u, lru_conv_w[j], lru_conv_b[j].reshape(1, -1), w_ax, b_ax, sp, B, S)
            w_out = lru_w_out[j].astype(BF16)
        else:
            tq = _tile(S, 512)
            w_in = fox_w_in[j]
            w_f = jnp.zeros((D, LANES), BF16).at[:, :H].set(w_in[:, 3 * D:].astype(BF16))
            qvt, k, f = _fox_proj(x2, g_mix, shift1, scale1, w_in.astype(BF16), w_f,
                                  fox_q_gain[j].reshape(1, hd), fox_k_gain[j].reshape(1, hd),
                                  B, S, H, hd, tq, tq // 2)
            caug = _cum_forget(f, fox_b_f[j], B, S, H)
            y = _fox_attn(qvt, k, caug, B, S, H, hd, tq)
            w_out = fox_w_out[j].astype(BF16)
        x2 = _out_proj(y, w_out, x2, gate1, S)
        x2 = _moe_layer(x2, g_ffn, shift2, scale2, gate2, moe_w_router[layer], moe_b_router[layer],
                        moe_w_gu, moe_b_gu, moe_w_down, moe_b_down, layer, S)
    return x2.reshape(B, S, D)
```
